```python
import math
import jax, jax.numpy as jnp
from jax import lax
import numpy as np

D_MODEL = 1024
BATCH = 4
SEQ = 4096
DEPTH = 4

CHUNK = 64
Q_BLOCK = 128
N_HEADS = 8
HEAD_DIM = 64
V_DIM = 2 * HEAD_DIM
QK_WIDTH = N_HEADS * 2 * HEAD_DIM
ATTN_WIDTH = N_HEADS * V_DIM
ROT_DIM = HEAD_DIM // 4
ROPE_THETA = 500000.0
CONV_WIDTH = D_MODEL
CONV_K = 3
D_FF = -(-8 * D_MODEL // (3 * 256)) * 256
IN_WIDTH = 3 * QK_WIDTH + 3 * CONV_WIDTH + 2 * D_MODEL
SPLITS = tuple(int(v) for v in np.cumsum([QK_WIDTH, QK_WIDTH, ATTN_WIDTH, CONV_WIDTH, CONV_WIDTH, CONV_WIDTH, D_MODEL]))
NORM_EPS = 1e-6
NEG_INF = -1e30

kernel_name = "chunk_causal_hybrid_diffattn_shortconv"


def rms_norm(x, w):
    xf = x.astype(jnp.float32)
    y = xf * lax.rsqrt(jnp.mean(xf * xf, axis=-1, keepdims=True) + NORM_EPS)
    return (y * w.astype(jnp.float32)).astype(x.dtype)


def rope_tables(positions):
    freqs = ROPE_THETA ** (-jnp.arange(0, ROT_DIM, 2, dtype=jnp.float32) / ROT_DIM)
    ang = positions.astype(jnp.float32)[..., None] * freqs
    return jnp.cos(ang), jnp.sin(ang)


def apply_partial_rope(t, cos, sin):
    c = cos[:, :, None, None, :]
    s = sin[:, :, None, None, :]
    rot = t[..., :ROT_DIM].astype(jnp.float32)
    x1, x2 = rot[..., : ROT_DIM // 2], rot[..., ROT_DIM // 2:]
    r = jnp.concatenate([x1 * c - x2 * s, x2 * c + x1 * s], axis=-1)
    return jnp.concatenate([r.astype(t.dtype), t[..., ROT_DIM:]], axis=-1)


def diff_attention(q, k, v, chunk_id, lam):
    b, s = q.shape[0], q.shape[1]
    nb = s // Q_BLOCK
    scale = HEAD_DIM ** -0.5
    kt = k.transpose(0, 2, 3, 1, 4)
    vt = v.transpose(0, 2, 1, 3)
    qb = q.reshape(b, nb, Q_BLOCK, N_HEADS, 2, HEAD_DIM).transpose(1, 0, 3, 4, 2, 5)
    qcb = chunk_id.reshape(b, nb, Q_BLOCK).transpose(1, 0, 2)

    def block(args):
        qi, qci = args
        sc = jnp.einsum('bhmqd,bhmkd->bhmqk', qi, kt).astype(jnp.float32) * scale
        allowed = chunk_id[:, None, :] <= qci[:, :, None]
        sc = jnp.where(allowed[:, None, None], sc, NEG_INF)
        p = jax.nn.softmax(sc, axis=-1)
        a = p[:, :, 0] - lam * p[:, :, 1]
        return jnp.einsum('bhqk,bhkd->bhqd', a.astype(vt.dtype), vt)

    o = lax.map(block, (qb, qcb))
    return o.transpose(1, 0, 3, 2, 4).reshape(b, s, N_HEADS, V_DIM)


def causal_dwconv(u, w):
    return lax.conv_general_dilated(
        u, w[:, None, :].astype(u.dtype), window_strides=(1,),
        padding=[(CONV_K - 1, 0)], dimension_numbers=('NWC', 'WIO', 'NWC'),
        feature_group_count=u.shape[-1])


def setup_inputs(seed: int = 0) -> dict:
    key = jax.random.key(seed)
    ks = jax.random.split(key, 20)
    f32 = jnp.float32

    def nrm(k, shape, fan_in):
        return jax.random.normal(k, shape, f32) * (fan_in ** -0.5)

    x = jax.random.normal(ks[0], (BATCH, SEQ, D_MODEL), f32)
    start = jax.random.randint(ks[1], (BATCH, 1), 0, 64, dtype=jnp.int32) * CHUNK
    positions = start + jnp.arange(SEQ, dtype=jnp.int32)[None, :]
    return {
        "x": x,
        "positions": positions,
        "mix_norm": 1.0 + 0.01 * jax.random.normal(ks[2], (DEPTH, D_MODEL), f32),
        "w_in": nrm(ks[3], (DEPTH, D_MODEL, IN_WIDTH), D_MODEL),
        "lambda_q1": 0.1 * jax.random.normal(ks[4], (DEPTH, HEAD_DIM), f32),
        "lambda_k1": 0.1 * jax.random.normal(ks[5], (DEPTH, HEAD_DIM), f32),
        "lambda_q2": 0.1 * jax.random.normal(ks[6], (DEPTH, HEAD_DIM), f32),
        "lambda_k2": 0.1 * jax.random.normal(ks[7], (DEPTH, HEAD_DIM), f32),
        "subln_w": 1.0 + 0.01 * jax.random.normal(ks[8], (DEPTH, V_DIM), f32),
        "conv_w": nrm(ks[9], (DEPTH, CONV_K, CONV_WIDTH), CONV_K),
        "w_branch_a": nrm(ks[10], (DEPTH, ATTN_WIDTH, D_MODEL), ATTN_WIDTH),
        "w_branch_b": nrm(ks[11], (DEPTH, CONV_WIDTH, D_MODEL), CONV_WIDTH),
        "w_out": nrm(ks[12], (DEPTH, D_MODEL, D_MODEL), D_MODEL),
        "ffn_norm": 1.0 + 0.01 * jax.random.normal(ks[13], (DEPTH, D_MODEL), f32),
        "w_gate": nrm(ks[14], (DEPTH, D_MODEL, D_FF), D_MODEL),
        "w_up": nrm(ks[15], (DEPTH, D_MODEL, D_FF), D_MODEL),
        "w_down": nrm(ks[16], (DEPTH, D_FF, D_MODEL), D_FF),
        "final_norm": 1.0 + 0.01 * jax.random.normal(ks[17], (D_MODEL,), f32),
    }


def reference(x, positions, mix_norm, w_in, lambda_q1, lambda_k1, lambda_q2, lambda_k2,
              subln_w, conv_w, w_branch_a, w_branch_b, w_out, ffn_norm,
              w_gate, w_up, w_down, final_norm):
    b, s, _ = x.shape
    cos, sin = rope_tables(positions)
    chunk_id = positions // CHUNK
    for l in range(DEPTH):
        lambda_init = 0.8 - 0.6 * math.exp(-0.3 * l)
        xn = rms_norm(x, mix_norm[l])
        proj = xn @ w_in[l]
        q, k, v, bg, cg, u, ga, gb = jnp.split(proj, SPLITS, axis=-1)
        q = apply_partial_rope(q.reshape(b, s, N_HEADS, 2, HEAD_DIM), cos, sin)
        k = apply_partial_rope(k.reshape(b, s, N_HEADS, 2, HEAD_DIM), cos, sin)
        v = v.reshape(b, s, N_HEADS, V_DIM)
        lam = (jnp.exp(jnp.sum(lambda_q1[l].astype(jnp.float32) * lambda_k1[l].astype(jnp.float32)))
               - jnp.exp(jnp.sum(lambda_q2[l].astype(jnp.float32) * lambda_k2[l].astype(jnp.float32)))
               + lambda_init)
        o = diff_attention(q, k, v, chunk_id, lam)
        o = rms_norm(o, subln_w[l]) * (1.0 - lambda_init)
        y_a = o.reshape(b, s, ATTN_WIDTH) @ w_branch_a[l]
        y_b = (bg * causal_dwconv(cg * u, conv_w[l])) @ w_branch_b[l]
        mixed = jax.nn.sigmoid(ga) * y_a + jax.nn.sigmoid(gb) * y_b
        x = x + mixed @ w_out[l]
        hn = rms_norm(x, ffn_norm[l])
        x = x + (jax.nn.silu(hn @ w_gate[l]) * (hn @ w_up[l])) @ w_down[l]
    return rms_norm(x, final_norm)
```

```python
import functools
import math

import jax
import jax.numpy as jnp
from jax import lax
from jax.experimental import pallas as pl
from jax.experimental.pallas import tpu as pltpu

D_MODEL = 1024
DEPTH = 4
CHUNK_SHIFT = 6
N_HEADS = 8
HEAD_DIM = 64
V_DIM = 2 * HEAD_DIM
ROT_DIM = HEAD_DIM // 4
ROPE_THETA = 500000.0
CONV_K = 3
D_FF = 2816
IN_WIDTH = 8 * D_MODEL
NORM_EPS = 1e-6
NEG_INF = -1e30

LANES = 128
SUBLANES = 8
VMEM_LIMIT = 56 * 1024 * 1024

TM_PROJ = 1024
TN_PROJ = 1024
TQ = 512
TK = 512
TM_MIX = 512
TM_FFN = 512
FF_CHUNK = 1408

_F32 = jnp.float32
_BF16 = jnp.bfloat16


def _rms(x, w):
    return x * lax.rsqrt(jnp.mean(x * x, axis=-1, keepdims=True) + NORM_EPS) * w


def _rope_table_kernel(ang_ref, cos_ref, sin_ref):
    ang = ang_ref[...]
    p = lax.broadcasted_iota(jnp.int32, ang.shape, 1) % HEAD_DIM
    c = jnp.cos(ang)
    s = jnp.sin(ang)
    cos_ref[...] = jnp.where(p < ROT_DIM, c, 1.0)
    sin_ref[...] = jnp.where(p < ROT_DIM // 2, -s, jnp.where(p < ROT_DIM, s, 0.0))


def _rope_tables(positions):
    m = positions.size
    freqs = ROPE_THETA ** (-jnp.arange(0, ROT_DIM, 2, dtype=_F32) / ROT_DIM)
    lane_freq = jnp.tile(freqs, LANES // (ROT_DIM // 2))
    ang = positions.reshape(m, 1).astype(_F32) * lane_freq[None, :]
    tm = 2048
    spec = pl.BlockSpec((tm, LANES), lambda i: (i, 0))
    return pl.pallas_call(
        _rope_table_kernel,
        out_shape=(jax.ShapeDtypeStruct((m, LANES), _F32),) * 2,
        grid=(m // tm,),
        in_specs=[spec],
        out_specs=(spec, spec),
        name="rope_tables",
    )(ang)


def _inproj_kernel(x_ref, nw_ref, w_ref, cos_ref, sin_ref, o_ref, xn_ref):
    j = pl.program_id(1)

    @pl.when(j == 0)
    def _():
        xn_ref[...] = _rms(x_ref[...], nw_ref[...]).astype(_BF16)

    acc = jnp.dot(xn_ref[...], w_ref[...], preferred_element_type=_F32)

    @pl.when(j < 2)
    def _():
        cos = cos_ref[...]
        sin = sin_ref[...]
        first_half = (lax.broadcasted_iota(jnp.int32, cos.shape, 1) % HEAD_DIM) < ROT_DIM // 2
        scale = jnp.where(j == 0, HEAD_DIM ** -0.5, 1.0).astype(_F32)
        for c in range(TN_PROJ // LANES):
            t = acc[:, c * LANES:(c + 1) * LANES]
            partner = jnp.where(first_half,
                                pltpu.roll(t, LANES - ROT_DIM // 2, axis=1),
                                pltpu.roll(t, ROT_DIM // 2, axis=1))
            r = (t * cos + partner * sin) * scale
            o_ref[:, c * LANES:(c + 1) * LANES] = r.astype(o_ref.dtype)

    @pl.when(j >= 2)
    def _():
        o_ref[...] = acc.astype(o_ref.dtype)


def _inproj(x, norm_w, w_bf16, cos_t, sin_t):
    m = x.shape[0]
    return pl.pallas_call(
        _inproj_kernel,
        out_shape=jax.ShapeDtypeStruct((m, IN_WIDTH), _BF16),
        grid=(m // TM_PROJ, IN_WIDTH // TN_PROJ),
        in_specs=[
            pl.BlockSpec((TM_PROJ, D_MODEL), lambda i, j: (i, 0)),
            pl.BlockSpec((1, D_MODEL), lambda i, j: (0, 0)),
            pl.BlockSpec((D_MODEL, TN_PROJ), lambda i, j: (0, j)),
            pl.BlockSpec((TM_PROJ, LANES), lambda i, j: (i, 0)),
            pl.BlockSpec((TM_PROJ, LANES), lambda i, j: (i, 0)),
        ],
        out_specs=pl.BlockSpec((TM_PROJ, TN_PROJ), lambda i, j: (i, j)),
        scratch_shapes=[pltpu.VMEM((TM_PROJ, D_MODEL), _BF16)],
        compiler_params=pltpu.CompilerParams(
            dimension_semantics=("arbitrary", "arbitrary"),
            vmem_limit_bytes=VMEM_LIMIT),
        name="inproj",
    )(x, norm_w.reshape(1, D_MODEL), w_bf16, cos_t, sin_t)


def _attn_kernel(kidx_ref, nfull_ref, npart_ref,
                 q_ref, k_ref, v_ref, qcid_ref, kcid_ref, lam_ref, subw_ref, o_ref,
                 m1_ref, l1_ref, a1_ref, m2_ref, l2_ref, a2_ref, *, lambda_init, nq):
    b = pl.program_id(0)
    qi = pl.program_id(2)
    row = b * nq + qi

    q = q_ref[...]
    lane = lax.broadcasted_iota(jnp.int32, q.shape, 1)
    zero = jnp.zeros_like(q)
    q_lo = jnp.where(lane < HEAD_DIM, q, zero)
    q_hi = jnp.where(lane >= HEAD_DIM, q, zero)

    for m_ref, l_ref, a_ref in ((m1_ref, l1_ref, a1_ref), (m2_ref, l2_ref, a2_ref)):
        m_ref[...] = jnp.full(m_ref.shape, NEG_INF, _F32)
        l_ref[...] = jnp.zeros(l_ref.shape, _F32)
        a_ref[...] = jnp.zeros(a_ref.shape, _F32)

    def block(it, masked):
        ki = kidx_ref[row, it]
        start = pl.multiple_of(ki * TK, TK)
        k = k_ref[pl.ds(start, TK), :]
        v = v_ref[pl.ds(start, TK), :]
        if masked:
            allowed = kcid_ref[ki] <= qcid_ref[...]
        for qm, m_ref, l_ref, a_ref in ((q_lo, m1_ref, l1_ref, a1_ref), (q_hi, m2_ref, l2_ref, a2_ref)):
            s = lax.dot_general(qm, k, (((1,), (1,)), ((), ())), preferred_element_type=_F32)
            if masked:
                s = jnp.where(allowed, s, NEG_INF)
            m_old = m_ref[...]
            m_new = jnp.maximum(m_old, jnp.max(s, axis=-1, keepdims=True))
            alpha = jnp.exp(m_old - m_new)
            p = jnp.exp(s - m_new)
            l_ref[...] = alpha * l_ref[...] + jnp.sum(p, axis=-1, keepdims=True)
            a_ref[...] = alpha * a_ref[...] + jnp.dot(p.astype(_BF16), v, preferred_element_type=_F32)
            m_ref[...] = m_new

    nfull = nfull_ref[row]
    npart = npart_ref[row]

    def full_body(it, carry):
        block(it, masked=False)
        return carry

    def part_body(it, carry):
        block(it, masked=True)
        return carry

    lax.fori_loop(0, nfull, full_body, 0)
    lax.fori_loop(nfull, nfull + npart, part_body, 0)

    lam_v = lam_ref[...]
    lam = (jnp.exp(jnp.sum(lam_v[0:1] * lam_v[1:2], axis=-1, keepdims=True))
           - jnp.exp(jnp.sum(lam_v[2:3] * lam_v[3:4], axis=-1, keepdims=True))
           + lambda_init)
    o = a1_ref[...] / l1_ref[...] - lam * (a2_ref[...] / l2_ref[...])
    o = _rms(o, subw_ref[...]) * (1.0 - lambda_init)
    o_ref[...] = o.astype(o_ref.dtype)


def _attn_schedule(positions):
    b, s = positions.shape
    nq, nk = s // TQ, s // TK
    cid = jnp.right_shift(positions, CHUNK_SHIFT)
    qc = cid.reshape(b, nq, TQ)
    kc = cid.reshape(b, nk, TK)
    qmin, qmax = qc.min(-1), qc.max(-1)
    kmin, kmax = kc.min(-1), kc.max(-1)
    need = kmin[:, None, :] <= qmax[:, :, None]
    full = kmax[:, None, :] <= qmin[:, :, None]
    part = need & ~full
    rank = jnp.where(full, 0, jnp.where(part, 1, 2))
    kidx = jnp.argsort(rank, axis=-1, stable=True).astype(jnp.int32)
    nfull = full.sum(-1).astype(jnp.int32)
    npart = part.sum(-1).astype(jnp.int32)
    return (kidx.reshape(b * nq, nk), nfull.reshape(b * nq), npart.reshape(b * nq),
            cid.reshape(b * s, 1), cid.reshape(b, nk, 1, TK))


def _attn(proj, sched, lam_params, subw, lambda_init, b, s):
    kidx, nfull, npart, qcid, kcid = sched
    nq, nk = s // TQ, s // TK
    kern = functools.partial(_attn_kernel, lambda_init=lambda_init, nq=nq)
    grid_spec = pltpu.PrefetchScalarGridSpec(
        num_scalar_prefetch=3,
        grid=(b, N_HEADS, nq),
        in_specs=[
            pl.BlockSpec((TQ, V_DIM), lambda bi, h, qi, *_: (bi * nq + qi, h)),
            pl.BlockSpec((s, V_DIM), lambda bi, h, qi, *_: (bi, N_HEADS + h)),
            pl.BlockSpec((s, V_DIM), lambda bi, h, qi, *_: (bi, 2 * N_HEADS + h)),
            pl.BlockSpec((TQ, 1), lambda bi, h, qi, *_: (bi * nq + qi, 0)),
            pl.BlockSpec((None, nk, 1, TK), lambda bi, h, qi, *_: (bi, 0, 0, 0)),
            pl.BlockSpec((4, HEAD_DIM), lambda bi, h, qi, *_: (0, 0)),
            pl.BlockSpec((1, V_DIM), lambda bi, h, qi, *_: (0, 0)),
        ],
        out_specs=pl.BlockSpec((TQ, V_DIM), lambda bi, h, qi, *_: (bi * nq + qi, h)),
        scratch_shapes=[
            pltpu.VMEM((TQ, 1), _F32), pltpu.VMEM((TQ, 1), _F32), pltpu.VMEM((TQ, V_DIM), _F32),
            pltpu.VMEM((TQ, 1), _F32), pltpu.VMEM((TQ, 1), _F32), pltpu.VMEM((TQ, V_DIM), _F32),
        ],
    )
    return pl.pallas_call(
        kern,
        out_shape=jax.ShapeDtypeStruct((b * s, N_HEADS * V_DIM), _BF16),
        grid_spec=grid_spec,
        compiler_params=pltpu.CompilerParams(
            dimension_semantics=("arbitrary", "arbitrary", "arbitrary"),
            vmem_limit_bytes=VMEM_LIMIT),
        name="attn",
    )(kidx, nfull, npart, proj, proj, proj, qcid, kcid, lam_params, subw.reshape(1, V_DIM))


def _mix_kernel(x_ref, o_ref, bg_ref, cg_ref, u_ref, ga_ref, gb_ref, cgh_ref, uh_ref,
                cw_ref, wa_ref, wb_ref, wo_ref, out_ref, ext_ref, *, tiles_per_seq):
    i = pl.program_id(0)
    halo = cgh_ref[...].astype(_F32) * uh_ref[...].astype(_F32)
    halo = jnp.where(i % tiles_per_seq == 0, 0.0, halo)
    ext_ref[0:SUBLANES, :] = halo
    ext_ref[SUBLANES:, :] = cg_ref[...].astype(_F32) * u_ref[...].astype(_F32)
    cw = cw_ref[...]
    conv = (cw[0:1] * ext_ref[SUBLANES - 2:SUBLANES - 2 + TM_MIX, :]
            + cw[1:2] * ext_ref[SUBLANES - 1:SUBLANES - 1 + TM_MIX, :]
            + cw[2:3] * ext_ref[SUBLANES:, :])
    yb_in = (bg_ref[...].astype(_F32) * conv).astype(_BF16)
    y_b = jnp.dot(yb_in, wb_ref[...], preferred_element_type=_F32)
    y_a = jnp.dot(o_ref[...], wa_ref[...], preferred_element_type=_F32)
    mixed = (jax.nn.sigmoid(ga_ref[...].astype(_F32)) * y_a
             + jax.nn.sigmoid(gb_ref[...].astype(_F32)) * y_b)
    out_ref[...] = x_ref[...] + jnp.dot(mixed.astype(_BF16), wo_ref[...], preferred_element_type=_F32)


def _mix(x, attn_o, proj, conv_w, wa, wb, wo, s):
    m = x.shape[0]
    tiles_per_seq = s // TM_MIX
    halo_blocks = TM_MIX // SUBLANES

    def col(c):
        return pl.BlockSpec((TM_MIX, D_MODEL), lambda i: (i, c))

    def halo(c):
        return pl.BlockSpec((SUBLANES, D_MODEL), lambda i: (jnp.maximum(i * halo_blocks - 1, 0), c))

    def whole(shape):
        return pl.BlockSpec(shape, lambda i: (0, 0))

    kern = functools.partial(_mix_kernel, tiles_per_seq=tiles_per_seq)
    return pl.pallas_call(
        kern,
        out_shape=jax.ShapeDtypeStruct((m, D_MODEL), _F32),
        grid=(m // TM_MIX,),
        in_specs=[col(0), col(0), col(3), col(4), col(5), col(6), col(7), halo(4), halo(5),
                  whole((CONV_K, D_MODEL)), whole((D_MODEL, D_MODEL)), whole((D_MODEL, D_MODEL)),
                  whole((D_MODEL, D_MODEL))],
        out_specs=col(0),
        scratch_shapes=[pltpu.VMEM((TM_MIX + SUBLANES, D_MODEL), _F32)],
        compiler_params=pltpu.CompilerParams(
            dimension_semantics=("arbitrary",), vmem_limit_bytes=VMEM_LIMIT),
        name="mix",
    )(x, attn_o, proj, proj, proj, proj, proj, proj, proj, conv_w, wa, wb, wo)


def _ffn_kernel(x_ref, nw_ref, wg_ref, wu_ref, wd_ref, fw_ref, out_ref, *, final_norm):
    x = x_ref[...]
    hn = _rms(x, nw_ref[...]).astype(_BF16)
    acc = x
    for c in range(D_FF // FF_CHUNK):
        sl = slice(c * FF_CHUNK, (c + 1) * FF_CHUNK)
        g = jnp.dot(hn, wg_ref[:, sl], preferred_element_type=_F32)
        u = jnp.dot(hn, wu_ref[:, sl], preferred_element_type=_F32)
        h = (jax.nn.silu(g) * u).astype(_BF16)
        acc = acc + jnp.dot(h, wd_ref[sl, :], preferred_element_type=_F32)
    if final_norm:
        acc = _rms(acc, fw_ref[...])
    out_ref[...] = acc


def _ffn(x, norm_w, wg, wu, wd, final_w, final_norm):
    m = x.shape[0]
    tile = pl.BlockSpec((TM_FFN, D_MODEL), lambda i: (i, 0))

    def whole(shape):
        return pl.BlockSpec(shape, lambda i: (0, 0), pipeline_mode=pl.Buffered(1))

    kern = functools.partial(_ffn_kernel, final_norm=final_norm)
    return pl.pallas_call(
        kern,
        out_shape=jax.ShapeDtypeStruct((m, D_MODEL), _F32),
        grid=(m // TM_FFN,),
        in_specs=[tile, whole((1, D_MODEL)), whole((D_MODEL, D_FF)), whole((D_MODEL, D_FF)),
                  whole((D_FF, D_MODEL)), whole((1, D_MODEL))],
        out_specs=tile,
        compiler_params=pltpu.CompilerParams(
            dimension_semantics=("arbitrary",), vmem_limit_bytes=VMEM_LIMIT),
        name="ffn",
    )(x, norm_w.reshape(1, D_MODEL), wg, wu, wd, final_w.reshape(1, D_MODEL))


def kernel(x, positions, mix_norm, w_in, lambda_q1, lambda_k1, lambda_q2, lambda_k2, subln_w, conv_w,
           w_branch_a, w_branch_b, w_out, ffn_norm, w_gate, w_up, w_down, final_norm):
    b, s, d = x.shape
    assert d == D_MODEL and s % TQ == 0 and s % TM_MIX == 0 and (b * s) % TM_PROJ == 0
    h = x.reshape(b * s, d)
    cos_t, sin_t = _rope_tables(positions)
    sched = _attn_schedule(positions)
    for l in range(DEPTH):
        lambda_init = 0.8 - 0.6 * math.exp(-0.3 * l)
        proj = _inproj(h, mix_norm[l], w_in[l].astype(_BF16), cos_t, sin_t)
        lam_params = jnp.stack([lambda_q1[l], lambda_k1[l], lambda_q2[l], lambda_k2[l]]).astype(_F32)
        attn_o = _attn(proj, sched, lam_params, subln_w[l], lambda_init, b, s)
        h = _mix(h, attn_o, proj, conv_w[l], w_branch_a[l].astype(_BF16),
                 w_branch_b[l].astype(_BF16), w_out[l].astype(_BF16), s)
        h = _ffn(h, ffn_norm[l], w_gate[l].astype(_BF16), w_up[l].astype(_BF16),
                 w_down[l].astype(_BF16), final_norm, l == DEPTH - 1)
    return h.reshape(b, s, d)
```

```python
import functools
import math

import jax
import jax.numpy as jnp
from jax import lax
from jax.experimental import pallas as pl
from jax.experimental.pallas import tpu as pltpu

D_MODEL = 1024
DEPTH = 4
CHUNK_SHIFT = 6
N_HEADS = 8
HEAD_DIM = 64
V_DIM = 2 * HEAD_DIM
ROT_DIM = HEAD_DIM // 4
ROPE_THETA = 500000.0
CONV_K = 3
D_FF = 2816
IN_WIDTH = 8 * D_MODEL
NORM_EPS = 1e-6
NEG_INF = -1e30
LOG2_E = 1.4426950408889634

LANES = 128
SUBLANES = 8
BF16_ROWS = 16
VMEM_LIMIT = 56 * 1024 * 1024

TM_PROJ = 1024
TN_PROJ = 1024
TQ = 512
TK = 512
VT_ROWS = V_DIM + BF16_ROWS
TM_MIX = 512
TM_FFN = 512
FF_CHUNK = 1408

_F32 = jnp.float32
_BF16 = jnp.bfloat16


def _rms(x, w):
    return x * lax.rsqrt(jnp.mean(x * x, axis=-1, keepdims=True) + NORM_EPS) * w


def _rope_table_kernel(ang_ref, cos_ref, sin_ref):
    ang = ang_ref[...]
    p = lax.broadcasted_iota(jnp.int32, ang.shape, 1) % HEAD_DIM
    c = jnp.cos(ang)
    s = jnp.sin(ang)
    cos_ref[...] = jnp.where(p < ROT_DIM, c, 1.0)
    sin_ref[...] = jnp.where(p < ROT_DIM // 2, -s, jnp.where(p < ROT_DIM, s, 0.0))


def _rope_tables(positions):
    m = positions.size
    freqs = ROPE_THETA ** (-jnp.arange(0, ROT_DIM, 2, dtype=_F32) / ROT_DIM)
    lane_freq = jnp.tile(freqs, LANES // (ROT_DIM // 2))
    ang = positions.reshape(m, 1).astype(_F32) * lane_freq[None, :]
    tm = 2048
    spec = pl.BlockSpec((tm, LANES), lambda i: (i, 0))
    return pl.pallas_call(
        _rope_table_kernel,
        out_shape=(jax.ShapeDtypeStruct((m, LANES), _F32),) * 2,
        grid=(m // tm,),
        in_specs=[spec],
        out_specs=(spec, spec),
        name="rope_tables",
    )(ang)


def _inproj_kernel(x_ref, nw_ref, w_ref, cos_ref, sin_ref, o_ref, xn_ref):
    j = pl.program_id(1)

    @pl.when(j == 0)
    def _():
        xn_ref[...] = _rms(x_ref[...], nw_ref[...]).astype(_BF16)

    acc = jnp.dot(xn_ref[...], w_ref[...], preferred_element_type=_F32)

    @pl.when(j < 2)
    def _():
        cos = cos_ref[...]
        sin = sin_ref[...]
        first_half = (lax.broadcasted_iota(jnp.int32, cos.shape, 1) % HEAD_DIM) < ROT_DIM // 2
        scale = jnp.where(j == 0, HEAD_DIM ** -0.5 * LOG2_E, 1.0).astype(_F32)
        for c in range(TN_PROJ // LANES):
            t = acc[:, c * LANES:(c + 1) * LANES]
            partner = jnp.where(first_half,
                                pltpu.roll(t, LANES - ROT_DIM // 2, axis=1),
                                pltpu.roll(t, ROT_DIM // 2, axis=1))
            r = (t * cos + partner * sin) * scale
            o_ref[:, c * LANES:(c + 1) * LANES] = r.astype(o_ref.dtype)

    @pl.when(j >= 2)
    def _():
        o_ref[...] = acc.astype(o_ref.dtype)


def _inproj(x, norm_w, w_bf16, cos_t, sin_t):
    m = x.shape[0]
    return pl.pallas_call(
        _inproj_kernel,
        out_shape=jax.ShapeDtypeStruct((m, IN_WIDTH), _BF16),
        grid=(m // TM_PROJ, IN_WIDTH // TN_PROJ),
        in_specs=[
            pl.BlockSpec((TM_PROJ, D_MODEL), lambda i, j: (i, 0)),
            pl.BlockSpec((1, D_MODEL), lambda i, j: (0, 0)),
            pl.BlockSpec((D_MODEL, TN_PROJ), lambda i, j: (0, j)),
            pl.BlockSpec((TM_PROJ, LANES), lambda i, j: (i, 0)),
            pl.BlockSpec((TM_PROJ, LANES), lambda i, j: (i, 0)),
        ],
        out_specs=pl.BlockSpec((TM_PROJ, TN_PROJ), lambda i, j: (i, j)),
        scratch_shapes=[pltpu.VMEM((TM_PROJ, D_MODEL), _BF16)],
        compiler_params=pltpu.CompilerParams(
            dimension_semantics=("arbitrary", "arbitrary"),
            vmem_limit_bytes=VMEM_LIMIT),
        name="inproj",
    )(x, norm_w.reshape(1, D_MODEL), w_bf16, cos_t, sin_t)


def _attn_kernel(kidx_ref, nfull_ref, npart_ref,
                 q_ref, k_ref, v_ref, qcid_ref, kcid_ref, lam_ref, subw_ref, o_ref,
                 vt_ref, m1_ref, a1_ref, m2_ref, a2_ref, *, lambda_init, nq, nk):
    b = pl.program_id(0)
    qi = pl.program_id(2)
    row = b * nq + qi

    @pl.when(qi == 0)
    def _():
        for j in range(nk):
            blk = v_ref[j * TK:(j + 1) * TK, :].astype(_F32)
            vt_ref[j, 0:V_DIM, :] = blk.T.astype(_BF16)
            vt_ref[j, V_DIM:, :] = jnp.ones((VT_ROWS - V_DIM, TK), _BF16)

    q = q_ref[...]
    lane = lax.broadcasted_iota(jnp.int32, q.shape, 1)
    zero = jnp.zeros_like(q)
    q_lo = jnp.where(lane < HEAD_DIM, q, zero)
    q_hi = jnp.where(lane >= HEAD_DIM, q, zero)

    for m_ref, a_ref in ((m1_ref, a1_ref), (m2_ref, a2_ref)):
        m_ref[...] = jnp.full(m_ref.shape, NEG_INF, _F32)
        a_ref[...] = jnp.zeros(a_ref.shape, _F32)

    def block(it, masked):
        ki = kidx_ref[row, it]
        start = pl.multiple_of(ki * TK, TK)
        k = k_ref[pl.ds(start, TK), :]
        vt = vt_ref[ki]
        if masked:
            allowed = kcid_ref[pl.ds(start, TK), :][:, 0:1] <= qcid_ref[...]
        for qm, m_ref, a_ref in ((q_lo, m1_ref, a1_ref), (q_hi, m2_ref, a2_ref)):
            s = lax.dot_general(k, qm, (((1,), (1,)), ((), ())), preferred_element_type=_F32)
            if masked:
                s = jnp.where(allowed, s, NEG_INF)
            m_old = m_ref[...]
            m_new = jnp.maximum(m_old, jnp.max(s, axis=0, keepdims=True))
            alpha = jnp.exp2(m_old - m_new)
            p = jnp.exp2(s - m_new).astype(_BF16)
            a_ref[...] = alpha * a_ref[...] + jnp.dot(vt, p, preferred_element_type=_F32)
            m_ref[...] = m_new

    nfull = nfull_ref[row]
    npart = npart_ref[row]

    def full_body(it, carry):
        block(it, masked=False)
        return carry

    def part_body(it, carry):
        block(it, masked=True)
        return carry

    lax.fori_loop(0, nfull, full_body, 0)
    lax.fori_loop(nfull, nfull + npart, part_body, 0)

    lam_v = lam_ref[...]
    lam = (jnp.exp(jnp.sum(lam_v[0:1] * lam_v[1:2], axis=-1, keepdims=True))
           - jnp.exp(jnp.sum(lam_v[2:3] * lam_v[3:4], axis=-1, keepdims=True))
           + lambda_init)
    o_t = (a1_ref[0:V_DIM, :] / a1_ref[V_DIM:V_DIM + 1, :]
           - lam * (a2_ref[0:V_DIM, :] / a2_ref[V_DIM:V_DIM + 1, :]))
    o_t = o_t * lax.rsqrt(jnp.mean(o_t * o_t, axis=0, keepdims=True) + NORM_EPS)
    o_ref[...] = (o_t.T * (subw_ref[...] * (1.0 - lambda_init))).astype(o_ref.dtype)


def _attn_schedule(positions):
    b, s = positions.shape
    nq, nk = s // TQ, s // TK
    cid = jnp.right_shift(positions, CHUNK_SHIFT)
    qc = cid.reshape(b, nq, TQ)
    kc = cid.reshape(b, nk, TK)
    qmin, qmax = qc.min(-1), qc.max(-1)
    kmin, kmax = kc.min(-1), kc.max(-1)
    need = kmin[:, None, :] <= qmax[:, :, None]
    full = kmax[:, None, :] <= qmin[:, :, None]
    part = need & ~full
    rank = jnp.where(full, 0, jnp.where(part, 1, 2))
    kidx = jnp.argsort(rank, axis=-1, stable=True).astype(jnp.int32)
    nfull = full.sum(-1).astype(jnp.int32)
    npart = part.sum(-1).astype(jnp.int32)
    kcid_lanes = jnp.broadcast_to(cid.reshape(b * s, 1), (b * s, LANES))
    return (kidx.reshape(b * nq, nk), nfull.reshape(b * nq), npart.reshape(b * nq),
            cid.reshape(b * nq, 1, TQ), kcid_lanes)


def _attn(proj, sched, lam_params, subw, lambda_init, b, s):
    kidx, nfull, npart, qcid, kcid = sched
    nq, nk = s // TQ, s // TK
    kern = functools.partial(_attn_kernel, lambda_init=lambda_init, nq=nq, nk=nk)
    grid_spec = pltpu.PrefetchScalarGridSpec(
        num_scalar_prefetch=3,
        grid=(b, N_HEADS, nq),
        in_specs=[
            pl.BlockSpec((TQ, V_DIM), lambda bi, h, qi, *_: (bi * nq + qi, h)),
            pl.BlockSpec((s, V_DIM), lambda bi, h, qi, *_: (bi, N_HEADS + h)),
            pl.BlockSpec((s, V_DIM), lambda bi, h, qi, *_: (bi, 2 * N_HEADS + h)),
            pl.BlockSpec((None, 1, TQ), lambda bi, h, qi, *_: (bi * nq + qi, 0, 0)),
            pl.BlockSpec((s, LANES), lambda bi, h, qi, *_: (bi, 0)),
            pl.BlockSpec((4, HEAD_DIM), lambda bi, h, qi, *_: (0, 0)),
            pl.BlockSpec((1, V_DIM), lambda bi, h, qi, *_: (0, 0)),
        ],
        out_specs=pl.BlockSpec((TQ, V_DIM), lambda bi, h, qi, *_: (bi * nq + qi, h)),
        scratch_shapes=[
            pltpu.VMEM((nk, VT_ROWS, TK), _BF16),
            pltpu.VMEM((1, TQ), _F32), pltpu.VMEM((VT_ROWS, TQ), _F32),
            pltpu.VMEM((1, TQ), _F32), pltpu.VMEM((VT_ROWS, TQ), _F32),
        ],
    )
    return pl.pallas_call(
        kern,
        out_shape=jax.ShapeDtypeStruct((b * s, N_HEADS * V_DIM), _BF16),
        grid_spec=grid_spec,
        compiler_params=pltpu.CompilerParams(
            dimension_semantics=("arbitrary", "arbitrary", "arbitrary"),
            vmem_limit_bytes=VMEM_LIMIT),
        name="attn",
    )(kidx, nfull, npart, proj, proj, proj, qcid, kcid, lam_params, subw.reshape(1, V_DIM))


def _mix_kernel(x_ref, o_ref, bg_ref, cg_ref, u_ref, ga_ref, gb_ref, cgh_ref, uh_ref,
                cw_ref, wa_ref, wb_ref, wo_ref, out_ref, ext_ref, *, tiles_per_seq):
    i = pl.program_id(0)
    halo = cgh_ref[...].astype(_F32) * uh_ref[...].astype(_F32)
    halo = jnp.where(i % tiles_per_seq == 0, 0.0, halo)
    ext_ref[0:SUBLANES, :] = halo
    ext_ref[SUBLANES:, :] = cg_ref[...].astype(_F32) * u_ref[...].astype(_F32)
    cw = cw_ref[...]
    conv = (cw[0:1] * ext_ref[SUBLANES - 2:SUBLANES - 2 + TM_MIX, :]
            + cw[1:2] * ext_ref[SUBLANES - 1:SUBLANES - 1 + TM_MIX, :]
            + cw[2:3] * ext_ref[SUBLANES:, :])
    yb_in = (bg_ref[...].astype(_F32) * conv).astype(_BF16)
    y_b = jnp.dot(yb_in, wb_ref[...], preferred_element_type=_F32)
    y_a = jnp.dot(o_ref[...], wa_ref[...], preferred_element_type=_F32)
    mixed = (jax.nn.sigmoid(ga_ref[...].astype(_F32)) * y_a
             + jax.nn.sigmoid(gb_ref[...].astype(_F32)) * y_b)
    out_ref[...] = x_ref[...] + jnp.dot(mixed.astype(_BF16), wo_ref[...], preferred_element_type=_F32)


def _mix(x, attn_o, proj, conv_w, wa, wb, wo, s):
    m = x.shape[0]
    tiles_per_seq = s // TM_MIX
    halo_blocks = TM_MIX // SUBLANES

    def col(c):
        return pl.BlockSpec((TM_MIX, D_MODEL), lambda i: (i, c))

    def halo(c):
        return pl.BlockSpec((SUBLANES, D_MODEL), lambda i: (jnp.maximum(i * halo_blocks - 1, 0), c))

    def whole(shape):
        return pl.BlockSpec(shape, lambda i: (0, 0))

    kern = functools.partial(_mix_kernel, tiles_per_seq=tiles_per_seq)
    return pl.pallas_call(
        kern,
        out_shape=jax.ShapeDtypeStruct((m, D_MODEL), _F32),
        grid=(m // TM_MIX,),
        in_specs=[col(0), col(0), col(3), col(4), col(5), col(6), col(7), halo(4), halo(5),
                  whole((CONV_K, D_MODEL)), whole((D_MODEL, D_MODEL)), whole((D_MODEL, D_MODEL)),
                  whole((D_MODEL, D_MODEL))],
        out_specs=col(0),
        scratch_shapes=[pltpu.VMEM((TM_MIX + SUBLANES, D_MODEL), _F32)],
        compiler_params=pltpu.CompilerParams(
            dimension_semantics=("arbitrary",), vmem_limit_bytes=VMEM_LIMIT),
        name="mix",
    )(x, attn_o, proj, proj, proj, proj, proj, proj, proj, conv_w, wa, wb, wo)


def _ffn_kernel(x_ref, nw_ref, wg_ref, wu_ref, wd_ref, fw_ref, out_ref, *, final_norm):
    x = x_ref[...]
    hn = _rms(x, nw_ref[...]).astype(_BF16)
    acc = x
    for c in range(D_FF // FF_CHUNK):
        sl = slice(c * FF_CHUNK, (c + 1) * FF_CHUNK)
        g = jnp.dot(hn, wg_ref[:, sl], preferred_element_type=_F32)
        u = jnp.dot(hn, wu_ref[:, sl], preferred_element_type=_F32)
        h = (jax.nn.silu(g) * u).astype(_BF16)
        acc = acc + jnp.dot(h, wd_ref[sl, :], preferred_element_type=_F32)
    if final_norm:
        acc = _rms(acc, fw_ref[...])
    out_ref[...] = acc


def _ffn(x, norm_w, wg, wu, wd, final_w, final_norm):
    m = x.shape[0]
    tile = pl.BlockSpec((TM_FFN, D_MODEL), lambda i: (i, 0))

    def whole(shape):
        return pl.BlockSpec(shape, lambda i: (0, 0), pipeline_mode=pl.Buffered(1))

    kern = functools.partial(_ffn_kernel, final_norm=final_norm)
    return pl.pallas_call(
        kern,
        out_shape=jax.ShapeDtypeStruct((m, D_MODEL), _F32),
        grid=(m // TM_FFN,),
        in_specs=[tile, whole((1, D_MODEL)), whole((D_MODEL, D_FF)), whole((D_MODEL, D_FF)),
                  whole((D_FF, D_MODEL)), whole((1, D_MODEL))],
        out_specs=tile,
        compiler_params=pltpu.CompilerParams(
            dimension_semantics=("arbitrary",), vmem_limit_bytes=VMEM_LIMIT),
        name="ffn",
    )(x, norm_w.reshape(1, D_MODEL), wg, wu, wd, final_w.reshape(1, D_MODEL))


def kernel(x, positions, mix_norm, w_in, lambda_q1, lambda_k1, lambda_q2, lambda_k2, subln_w, conv_w,
           w_branch_a, w_branch_b, w_out, ffn_norm, w_gate, w_up, w_down, final_norm):
    b, s, d = x.shape
    assert d == D_MODEL and s % TQ == 0 and s % TM_MIX == 0 and (b * s) % TM_PROJ == 0
    h = x.reshape(b * s, d)
    cos_t, sin_t = _rope_tables(positions)
    sched = _attn_schedule(positions)
    for l in range(DEPTH):
        lambda_init = 0.8 - 0.6 * math.exp(-0.3 * l)
        proj = _inproj(h, mix_norm[l], w_in[l].astype(_BF16), cos_t, sin_t)
        lam_params = jnp.stack([lambda_q1[l], lambda_k1[l], lambda_q2[l], lambda_k2[l]]).astype(_F32)
        attn_o = _attn(proj, sched, lam_params, subln_w[l], lambda_init, b, s)
        h = _mix(h, attn_o, proj, conv_w[l], w_branch_a[l].astype(_BF16),
                 w_branch_b[l].astype(_BF16), w_out[l].astype(_BF16), s)
        h = _ffn(h, ffn_norm[l], w_gate[l].astype(_BF16), w_up[l].astype(_BF16),
                 w_down[l].astype(_BF16), final_norm, l == DEPTH - 1)
    return h.reshape(b, s, d)
```

```python
import functools
import math

import jax
import jax.numpy as jnp
from jax import lax
from jax.experimental import pallas as pl
from jax.experimental.pallas import tpu as pltpu

D_MODEL = 1024
DEPTH = 4
CHUNK_SHIFT = 6
N_HEADS = 8
HEAD_DIM = 64
V_DIM = 2 * HEAD_DIM
ROT_DIM = HEAD_DIM // 4
ROPE_THETA = 500000.0
CONV_K = 3
D_FF = 2816
IN_WIDTH = 8 * D_MODEL
NORM_EPS = 1e-6
NEG_INF = -1e30
LOG2_E = 1.4426950408889634

LANES = 128
SUBLANES = 8
BF16_ROWS = 16
VMEM_LIMIT = 56 * 1024 * 1024

TM_PROJ = 512
TN_PROJ = 512
TQ = 512
TK = 512
VT_ROWS = V_DIM + BF16_ROWS
TM_MIX = 512
TM_FFN = 512
FF_CHUNK = 1408

_F32 = jnp.float32
_BF16 = jnp.bfloat16


def _rms(x, w):
    return x * lax.rsqrt(jnp.mean(x * x, axis=-1, keepdims=True) + NORM_EPS) * w


def _rope_table_kernel(ang_ref, cos_ref, sin_ref):
    ang = ang_ref[...]
    p = lax.broadcasted_iota(jnp.int32, ang.shape, 1) % HEAD_DIM
    c = jnp.cos(ang)
    s = jnp.sin(ang)
    cos_ref[...] = jnp.where(p < ROT_DIM, c, 1.0)
    sin_ref[...] = jnp.where(p < ROT_DIM // 2, -s, jnp.where(p < ROT_DIM, s, 0.0))


def _rope_tables(positions):
    m = positions.size
    freqs = ROPE_THETA ** (-jnp.arange(0, ROT_DIM, 2, dtype=_F32) / ROT_DIM)
    lane_freq = jnp.tile(freqs, LANES // (ROT_DIM // 2))
    ang = positions.reshape(m, 1).astype(_F32) * lane_freq[None, :]
    tm = 2048
    spec = pl.BlockSpec((tm, LANES), lambda i: (i, 0))
    return pl.pallas_call(
        _rope_table_kernel,
        out_shape=(jax.ShapeDtypeStruct((m, LANES), _F32),) * 2,
        grid=(m // tm,),
        in_specs=[spec],
        out_specs=(spec, spec),
        name="rope_tables",
    )(ang)


def _inproj_kernel(x_ref, nw_ref, w_ref, cos_ref, sin_ref, o_ref):
    xn = _rms(x_ref[...], nw_ref[...]).astype(_BF16)
    cos = cos_ref[...]
    sin = sin_ref[...]
    first_half = (lax.broadcasted_iota(jnp.int32, cos.shape, 1) % HEAD_DIM) < ROT_DIM // 2
    q_scale = HEAD_DIM ** -0.5 * LOG2_E
    for jt in range(IN_WIDTH // TN_PROJ):
        cols = slice(jt * TN_PROJ, (jt + 1) * TN_PROJ)
        acc = jnp.dot(xn, w_ref[:, cols], preferred_element_type=_F32)
        if jt * TN_PROJ >= 2 * D_MODEL:
            o_ref[:, cols] = acc.astype(o_ref.dtype)
            continue
        if jt * TN_PROJ < D_MODEL:
            cs, sn = cos * q_scale, sin * q_scale
        else:
            cs, sn = cos, sin
        for c in range(TN_PROJ // LANES):
            t = acc[:, c * LANES:(c + 1) * LANES]
            partner = jnp.where(first_half,
                                pltpu.roll(t, LANES - ROT_DIM // 2, axis=1),
                                pltpu.roll(t, ROT_DIM // 2, axis=1))
            lo = jt * TN_PROJ + c * LANES
            o_ref[:, lo:lo + LANES] = (t * cs + partner * sn).astype(o_ref.dtype)


def _inproj(x, norm_w, w_bf16, cos_t, sin_t):
    m = x.shape[0]

    def whole(shape):
        return pl.BlockSpec(shape, lambda i: (0, 0), pipeline_mode=pl.Buffered(1))

    return pl.pallas_call(
        _inproj_kernel,
        out_shape=jax.ShapeDtypeStruct((m, IN_WIDTH), _BF16),
        grid=(m // TM_PROJ,),
        in_specs=[
            pl.BlockSpec((TM_PROJ, D_MODEL), lambda i: (i, 0)),
            whole((1, D_MODEL)),
            whole((D_MODEL, IN_WIDTH)),
            pl.BlockSpec((TM_PROJ, LANES), lambda i: (i, 0)),
            pl.BlockSpec((TM_PROJ, LANES), lambda i: (i, 0)),
        ],
        out_specs=pl.BlockSpec((TM_PROJ, IN_WIDTH), lambda i: (i, 0)),
        compiler_params=pltpu.CompilerParams(
            dimension_semantics=("arbitrary",), vmem_limit_bytes=VMEM_LIMIT),
        name="inproj",
    )(x, norm_w.reshape(1, D_MODEL), w_bf16, cos_t, sin_t)


def _attn_kernel(qf_ref, kf_ref, nf_ref, qp_ref, kp_ref, np_ref,
                 q_ref, k_ref, v_ref, qcid_ref, kcid_ref, lam_ref, subw_ref, o_ref,
                 vt_ref, s_ref, mb_ref, m_ref, acc_ref, *, lambda_init, nq, nk):
    b = pl.program_id(0)

    for j in range(nk):
        blk = v_ref[j * TK:(j + 1) * TK, :].astype(_F32)
        vt_ref[j, 0:V_DIM, :] = blk.T.astype(_BF16)
        vt_ref[j, V_DIM:, :] = jnp.ones((VT_ROWS - V_DIM, TK), _BF16)
    m_ref[...] = jnp.full(m_ref.shape, NEG_INF, _F32)
    acc_ref[...] = jnp.zeros(acc_ref.shape, _F32)

    lane = lax.broadcasted_iota(jnp.int32, (TQ, V_DIM), 1)

    def load_pair(qi, ki, masked):
        q = q_ref[pl.ds(pl.multiple_of(qi * TQ, TQ), TQ), :]
        zero = jnp.zeros_like(q)
        qms = (jnp.where(lane < HEAD_DIM, q, zero), jnp.where(lane >= HEAD_DIM, q, zero))
        ks = pl.multiple_of(ki * TK, TK)
        k = k_ref[pl.ds(ks, TK), :]
        allowed = None
        if masked:
            kc = kcid_ref[pl.ds(ks, TK), :]
            allowed = jnp.concatenate([kc] * (TQ // LANES), axis=1) <= qcid_ref[qi]
        return k, qms, allowed

    def scores(j, k, qms, allowed):
        s = lax.dot_general(k, qms[j], (((1,), (1,)), ((), ())), preferred_element_type=_F32)
        if allowed is not None:
            s = jnp.where(allowed, s, NEG_INF)
        s_ref[j] = s
        mb_ref[j] = jnp.max(s, axis=0, keepdims=True)

    def run_list(ql_ref, kl_ref, n, masked):
        @pl.when(n > 0)
        def _():
            k0, qms0, allowed0 = load_pair(ql_ref[b, 0], kl_ref[b, 0], masked)
            for j in range(2):
                scores(j, k0, qms0, allowed0)

            def body(t, carry):
                qi = ql_ref[b, t]
                vt = vt_ref[kl_ref[b, t]]
                tn = jnp.minimum(t + 1, n - 1)
                kn, qmsn, allowedn = load_pair(ql_ref[b, tn], kl_ref[b, tn], masked)
                for j in range(2):
                    m_old = m_ref[j, qi]
                    m_new = jnp.maximum(m_old, mb_ref[j])
                    alpha = jnp.exp2(m_old - m_new)
                    p = jnp.exp2(s_ref[j] - m_new).astype(_BF16)
                    scores(j, kn, qmsn, allowedn)
                    acc_ref[j, qi] = alpha * acc_ref[j, qi] + jnp.dot(vt, p, preferred_element_type=_F32)
                    m_ref[j, qi] = m_new
                return carry

            lax.fori_loop(0, n, body, 0)

    run_list(qf_ref, kf_ref, nf_ref[b], masked=False)
    run_list(qp_ref, kp_ref, np_ref[b], masked=True)

    lam_v = lam_ref[...]
    lam = (jnp.exp(jnp.sum(lam_v[0:1] * lam_v[1:2], axis=-1, keepdims=True))
           - jnp.exp(jnp.sum(lam_v[2:3] * lam_v[3:4], axis=-1, keepdims=True))
           + lambda_init)
    out_w = subw_ref[...] * (1.0 - lambda_init)

    def finish(qi, carry):
        a1 = acc_ref[0, qi]
        a2 = acc_ref[1, qi]
        o_t = (a1[0:V_DIM] * (1.0 / a1[V_DIM:V_DIM + 1])
               - a2[0:V_DIM] * (lam / a2[V_DIM:V_DIM + 1]))
        o_t = o_t * lax.rsqrt(jnp.mean(o_t * o_t, axis=0, keepdims=True) + NORM_EPS)
        o_ref[pl.ds(pl.multiple_of(qi * TQ, TQ), TQ), :] = (o_t.T * out_w).astype(o_ref.dtype)
        return carry

    lax.fori_loop(0, nq, finish, 0)


def _attn_schedule(positions):
    b, s = positions.shape
    nq, nk = s // TQ, s // TK
    cid = jnp.right_shift(positions, CHUNK_SHIFT)
    qc = cid.reshape(b, nq, TQ)
    kc = cid.reshape(b, nk, TK)
    qmin, qmax = qc.min(-1), qc.max(-1)
    kmin, kmax = kc.min(-1), kc.max(-1)
    need = kmin[:, None, :] <= qmax[:, :, None]
    full = kmax[:, None, :] <= qmin[:, :, None]
    part = need & ~full

    def pair_list(sel):
        sel = sel.reshape(b, nq * nk)
        order = jnp.argsort(~sel, axis=-1, stable=True).astype(jnp.int32)
        return order // nk, order % nk, sel.sum(-1).astype(jnp.int32)

    kcid_lanes = jnp.broadcast_to(cid.reshape(b * s, 1), (b * s, LANES))
    return pair_list(full) + pair_list(part) + (cid.reshape(b, nq, 1, TQ), kcid_lanes)


def _attn(proj, sched, lam_params, subw, lambda_init, b, s):
    qcid, kcid = sched[6:]
    nq, nk = s // TQ, s // TK
    kern = functools.partial(_attn_kernel, lambda_init=lambda_init, nq=nq, nk=nk)
    grid_spec = pltpu.PrefetchScalarGridSpec(
        num_scalar_prefetch=6,
        grid=(b, N_HEADS),
        in_specs=[
            pl.BlockSpec((s, V_DIM), lambda bi, h, *_: (bi, h)),
            pl.BlockSpec((s, V_DIM), lambda bi, h, *_: (bi, N_HEADS + h)),
            pl.BlockSpec((s, V_DIM), lambda bi, h, *_: (bi, 2 * N_HEADS + h)),
            pl.BlockSpec((None, nq, 1, TQ), lambda bi, h, *_: (bi, 0, 0, 0)),
            pl.BlockSpec((s, LANES), lambda bi, h, *_: (bi, 0)),
            pl.BlockSpec((4, HEAD_DIM), lambda bi, h, *_: (0, 0)),
            pl.BlockSpec((1, V_DIM), lambda bi, h, *_: (0, 0)),
        ],
        out_specs=pl.BlockSpec((s, V_DIM), lambda bi, h, *_: (bi, h)),
        scratch_shapes=[
            pltpu.VMEM((nk, VT_ROWS, TK), _BF16),
            pltpu.VMEM((2, TK, TQ), _F32),
            pltpu.VMEM((2, 1, TQ), _F32),
            pltpu.VMEM((2, nq, 1, TQ), _F32),
            pltpu.VMEM((2, nq, VT_ROWS, TQ), _F32),
        ],
    )
    return pl.pallas_call(
        kern,
        out_shape=jax.ShapeDtypeStruct((b * s, N_HEADS * V_DIM), _BF16),
        grid_spec=grid_spec,
        compiler_params=pltpu.CompilerParams(
            dimension_semantics=("arbitrary", "arbitrary"),
            vmem_limit_bytes=VMEM_LIMIT),
        name="attn",
    )(*sched[:6], proj, proj, proj, qcid, kcid, lam_params, subw.reshape(1, V_DIM))


def _mix_kernel(x_ref, o_ref, bg_ref, cg_ref, u_ref, ga_ref, gb_ref, cgh_ref, uh_ref,
                cw_ref, wa_ref, wb_ref, wo_ref, out_ref, ext_ref, *, tiles_per_seq):
    i = pl.program_id(0)
    halo = cgh_ref[...].astype(_F32) * uh_ref[...].astype(_F32)
    halo = jnp.where(i % tiles_per_seq == 0, 0.0, halo)
    ext_ref[0:SUBLANES, :] = halo
    ext_ref[SUBLANES:, :] = cg_ref[...].astype(_F32) * u_ref[...].astype(_F32)
    cw = cw_ref[...]
    conv = (cw[0:1] * ext_ref[SUBLANES - 2:SUBLANES - 2 + TM_MIX, :]
            + cw[1:2] * ext_ref[SUBLANES - 1:SUBLANES - 1 + TM_MIX, :]
            + cw[2:3] * ext_ref[SUBLANES:, :])
    yb_in = (bg_ref[...].astype(_F32) * conv).astype(_BF16)
    y_b = jnp.dot(yb_in, wb_ref[...], preferred_element_type=_F32)
    y_a = jnp.dot(o_ref[...], wa_ref[...], preferred_element_type=_F32)
    mixed = (jax.nn.sigmoid(ga_ref[...].astype(_F32)) * y_a
             + jax.nn.sigmoid(gb_ref[...].astype(_F32)) * y_b)
    out_ref[...] = x_ref[...] + jnp.dot(mixed.astype(_BF16), wo_ref[...], preferred_element_type=_F32)


def _mix(x, attn_o, proj, conv_w, wa, wb, wo, s):
    m = x.shape[0]
    tiles_per_seq = s // TM_MIX
    halo_blocks = TM_MIX // SUBLANES

    def col(c):
        return pl.BlockSpec((TM_MIX, D_MODEL), lambda i: (i, c))

    def halo(c):
        return pl.BlockSpec((SUBLANES, D_MODEL), lambda i: (jnp.maximum(i * halo_blocks - 1, 0), c))

    def whole(shape):
        return pl.BlockSpec(shape, lambda i: (0, 0))

    kern = functools.partial(_mix_kernel, tiles_per_seq=tiles_per_seq)
    return pl.pallas_call(
        kern,
        out_shape=jax.ShapeDtypeStruct((m, D_MODEL), _F32),
        grid=(m // TM_MIX,),
        in_specs=[col(0), col(0), col(3), col(4), col(5), col(6), col(7), halo(4), halo(5),
                  whole((CONV_K, D_MODEL)), whole((D_MODEL, D_MODEL)), whole((D_MODEL, D_MODEL)),
                  whole((D_MODEL, D_MODEL))],
        out_specs=col(0),
        scratch_shapes=[pltpu.VMEM((TM_MIX + SUBLANES, D_MODEL), _F32)],
        compiler_params=pltpu.CompilerParams(
            dimension_semantics=("arbitrary",), vmem_limit_bytes=VMEM_LIMIT),
        name="mix",
    )(x, attn_o, proj, proj, proj, proj, proj, proj, proj, conv_w, wa, wb, wo)


def _ffn_kernel(x_ref, nw_ref, wg_ref, wu_ref, wd_ref, fw_ref, out_ref, *, final_norm):
    x = x_ref[...]
    hn = _rms(x, nw_ref[...]).astype(_BF16)
    acc = x
    for c in range(D_FF // FF_CHUNK):
        sl = slice(c * FF_CHUNK, (c + 1) * FF_CHUNK)
        g = jnp.dot(hn, wg_ref[:, sl], preferred_element_type=_F32)
        u = jnp.dot(hn, wu_ref[:, sl], preferred_element_type=_F32)
        h = (jax.nn.silu(g) * u).astype(_BF16)
        acc = acc + jnp.dot(h, wd_ref[sl, :], preferred_element_type=_F32)
    if final_norm:
        acc = _rms(acc, fw_ref[...])
    out_ref[...] = acc


def _ffn(x, norm_w, wg, wu, wd, final_w, final_norm):
    m = x.shape[0]
    tile = pl.BlockSpec((TM_FFN, D_MODEL), lambda i: (i, 0))

    def whole(shape):
        return pl.BlockSpec(shape, lambda i: (0, 0), pipeline_mode=pl.Buffered(1))

    kern = functools.partial(_ffn_kernel, final_norm=final_norm)
    return pl.pallas_call(
        kern,
        out_shape=jax.ShapeDtypeStruct((m, D_MODEL), _F32),
        grid=(m // TM_FFN,),
        in_specs=[tile, whole((1, D_MODEL)), whole((D_MODEL, D_FF)), whole((D_MODEL, D_FF)),
                  whole((D_FF, D_MODEL)), whole((1, D_MODEL))],
        out_specs=tile,
        compiler_params=pltpu.CompilerParams(
            dimension_semantics=("arbitrary",), vmem_limit_bytes=VMEM_LIMIT),
        name="ffn",
    )(x, norm_w.reshape(1, D_MODEL), wg, wu, wd, final_w.reshape(1, D_MODEL))


def kernel(x, positions, mix_norm, w_in, lambda_q1, lambda_k1, lambda_q2, lambda_k2, subln_w, conv_w,
           w_branch_a, w_branch_b, w_out, ffn_norm, w_gate, w_up, w_down, final_norm):
    b, s, d = x.shape
    assert d == D_MODEL and s % TQ == 0 and s % TM_MIX == 0 and (b * s) % TM_PROJ == 0
    h = x.reshape(b * s, d)
    cos_t, sin_t = _rope_tables(positions)
    sched = _attn_schedule(positions)
    for l in range(DEPTH):
        lambda_init = 0.8 - 0.6 * math.exp(-0.3 * l)
        proj = _inproj(h, mix_norm[l], w_in[l].astype(_BF16), cos_t, sin_t)
        lam_params = jnp.stack([lambda_q1[l], lambda_k1[l], lambda_q2[l], lambda_k2[l]]).astype(_F32)
        attn_o = _attn(proj, sched, lam_params, subln_w[l], lambda_init, b, s)
        h = _mix(h, attn_o, proj, conv_w[l], w_branch_a[l].astype(_BF16),
                 w_branch_b[l].astype(_BF16), w_out[l].astype(_BF16), s)
        h = _ffn(h, ffn_norm[l], w_gate[l].astype(_BF16), w_up[l].astype(_BF16),
                 w_down[l].astype(_BF16), final_norm, l == DEPTH - 1)
    return h.reshape(b, s, d)
```

```python
import functools
import math

import jax
import jax.numpy as jnp
from jax import lax
from jax.experimental import pallas as pl
from jax.experimental.pallas import tpu as pltpu

D_MODEL = 1024
DEPTH = 4
CHUNK_SHIFT = 6
N_HEADS = 8
HEAD_DIM = 64
V_DIM = 2 * HEAD_DIM
ROT_DIM = HEAD_DIM // 4
ROPE_THETA = 500000.0
CONV_K = 3
D_FF = 2816
IN_WIDTH = 8 * D_MODEL
PROJ_WIDTH = IN_WIDTH - D_MODEL
NORM_EPS = 1e-6
NEG_INF = -1e30
LOG2_E = 1.4426950408889634

LANES = 128
SUBLANES = 8
BF16_ROWS = 16
VMEM_LIMIT = 56 * 1024 * 1024

TM_PROJ = 512
TN_PROJ = 512
TQ = 512
TK = 512
VT_ROWS = V_DIM + BF16_ROWS
PAIR_UNROLL = 2
TM_MIX = 512
TM_FFN = 512
FF_CHUNK = 1408

_F32 = jnp.float32
_BF16 = jnp.bfloat16


def _rms(x, w):
    return x * lax.rsqrt(jnp.mean(x * x, axis=-1, keepdims=True) + NORM_EPS) * w


def _layer_spec(shape, layer):
    return pl.BlockSpec((None,) + shape, lambda *_: (layer,) + (0,) * len(shape),
                        pipeline_mode=pl.Buffered(1))


def _rope_table_kernel(ang_ref, cos_ref, sin_ref):
    ang = ang_ref[...]
    p = lax.broadcasted_iota(jnp.int32, ang.shape, 1) % HEAD_DIM
    c = jnp.cos(ang)
    s = jnp.sin(ang)
    cos_ref[...] = jnp.where(p < ROT_DIM, c, 1.0)
    sin_ref[...] = jnp.where(p < ROT_DIM // 2, -s, jnp.where(p < ROT_DIM, s, 0.0))


def _rope_tables(positions):
    m = positions.size
    freqs = ROPE_THETA ** (-jnp.arange(0, ROT_DIM, 2, dtype=_F32) / ROT_DIM)
    lane_freq = jnp.tile(freqs, LANES // (ROT_DIM // 2))
    ang = positions.reshape(m, 1).astype(_F32) * lane_freq[None, :]
    tm = 2048
    spec = pl.BlockSpec((tm, LANES), lambda i: (i, 0))
    return pl.pallas_call(
        _rope_table_kernel,
        out_shape=(jax.ShapeDtypeStruct((m, LANES), _F32),) * 2,
        grid=(m // tm,),
        in_specs=[spec],
        out_specs=(spec, spec),
        name="rope_tables",
    )(ang)


def _inproj_kernel(x_ref, nw_ref, w_ref, cos_ref, sin_ref, o_ref, vt_ref):
    xn = _rms(x_ref[...], nw_ref[...]).astype(_BF16)
    cos = cos_ref[...]
    sin = sin_ref[...]
    first_half = (lax.broadcasted_iota(jnp.int32, cos.shape, 1) % HEAD_DIM) < ROT_DIM // 2
    q_scale = HEAD_DIM ** -0.5 * LOG2_E
    for jt in range(IN_WIDTH // TN_PROJ):
        lo = jt * TN_PROJ
        acc = jnp.dot(xn, w_ref[:, lo:lo + TN_PROJ], preferred_element_type=_F32)
        if lo >= 3 * D_MODEL:
            o_ref[:, lo - D_MODEL:lo - D_MODEL + TN_PROJ] = acc.astype(o_ref.dtype)
        elif lo >= 2 * D_MODEL:
            vt_ref[0, lo - 2 * D_MODEL:lo - 2 * D_MODEL + TN_PROJ, :] = acc.T.astype(vt_ref.dtype)
        else:
            cs, sn = (cos * q_scale, sin * q_scale) if lo < D_MODEL else (cos, sin)
            for c in range(TN_PROJ // LANES):
                t = acc[:, c * LANES:(c + 1) * LANES]
                partner = jnp.where(first_half,
                                    pltpu.roll(t, LANES - ROT_DIM // 2, axis=1),
                                    pltpu.roll(t, ROT_DIM // 2, axis=1))
                o_ref[:, lo + c * LANES:lo + (c + 1) * LANES] = (t * cs + partner * sn).astype(o_ref.dtype)


def _inproj(x, norm_w, w_bf16, layer, cos_t, sin_t):
    m = x.shape[0]
    assert TM_PROJ == TK
    return pl.pallas_call(
        _inproj_kernel,
        out_shape=(jax.ShapeDtypeStruct((m, PROJ_WIDTH), _BF16),
                   jax.ShapeDtypeStruct((m // TK, D_MODEL, TK), _BF16)),
        grid=(m // TM_PROJ,),
        in_specs=[
            pl.BlockSpec((TM_PROJ, D_MODEL), lambda i: (i, 0)),
            _layer_spec((1, D_MODEL), layer),
            _layer_spec((D_MODEL, IN_WIDTH), layer),
            pl.BlockSpec((TM_PROJ, LANES), lambda i: (i, 0)),
            pl.BlockSpec((TM_PROJ, LANES), lambda i: (i, 0)),
        ],
        out_specs=(pl.BlockSpec((TM_PROJ, PROJ_WIDTH), lambda i: (i, 0)),
                   pl.BlockSpec((1, D_MODEL, TK), lambda i: (i, 0, 0))),
        compiler_params=pltpu.CompilerParams(
            dimension_semantics=("arbitrary",), vmem_limit_bytes=VMEM_LIMIT),
        name="inproj",
    )(x, norm_w, w_bf16, cos_t, sin_t)


def _attn_kernel(qf_ref, kf_ref, sf_ref, nf_ref, qp_ref, kp_ref, sp_ref, np_ref,
                 q_ref, k_ref, vt_ref, qcid_ref, kcid_ref, lam_ref, subw_ref, o_ref,
                 s_ref, mb_ref, m_ref, acc_ref, *, lambda_init, nq):
    b = pl.program_id(0)

    m_ref[...] = jnp.full(m_ref.shape, NEG_INF, _F32)
    acc_ref[...] = jnp.zeros(acc_ref.shape, _F32)

    lane = lax.broadcasted_iota(jnp.int32, (TQ, V_DIM), 1)
    ones_rows = jnp.ones((VT_ROWS - V_DIM, TK), _BF16)

    def load_pair(qi, ki, masked):
        q = q_ref[pl.ds(pl.multiple_of(qi * TQ, TQ), TQ), :]
        zero = jnp.zeros_like(q)
        qms = (jnp.where(lane < HEAD_DIM, q, zero), jnp.where(lane >= HEAD_DIM, q, zero))
        ks = pl.multiple_of(ki * TK, TK)
        k = k_ref[pl.ds(ks, TK), :]
        allowed = None
        if masked:
            kc = kcid_ref[pl.ds(ks, TK), :]
            allowed = jnp.concatenate([kc] * (TQ // LANES), axis=1) <= qcid_ref[qi]
        return k, qms, allowed

    def scores(j, k, qms, allowed):
        s = lax.dot_general(k, qms[j], (((1,), (1,)), ((), ())), preferred_element_type=_F32)
        if allowed is not None:
            s = jnp.where(allowed, s, NEG_INF)
        s_ref[j] = s
        mb_ref[j] = jnp.max(s, axis=0, keepdims=True)

    def run_list(ql_ref, kl_ref, sl_ref, n, masked):
        @pl.when(n > 0)
        def _():
            k0, qms0, allowed0 = load_pair(ql_ref[b, 0], kl_ref[b, 0], masked)
            for j in range(2):
                scores(j, k0, qms0, allowed0)

            def body(i, carry):
                for u in range(PAIR_UNROLL):
                    t = i * PAIR_UNROLL + u
                    slot = sl_ref[b, t]
                    vt = jnp.concatenate([vt_ref[kl_ref[b, t]], ones_rows], axis=0)
                    kn, qmsn, allowedn = load_pair(ql_ref[b, t + 1], kl_ref[b, t + 1], masked)
                    for j in range(2):
                        m_old = m_ref[j, slot]
                        m_new = jnp.maximum(m_old, mb_ref[j])
                        alpha = jnp.exp2(m_old - m_new)
                        p = jnp.exp2(s_ref[j] - m_new).astype(_BF16)
                        scores(j, kn, qmsn, allowedn)
                        acc_ref[j, slot] = (alpha * acc_ref[j, slot]
                                            + jnp.dot(vt, p, preferred_element_type=_F32))
                        m_ref[j, slot] = m_new
                return carry

            lax.fori_loop(0, (n + PAIR_UNROLL - 1) // PAIR_UNROLL, body, 0)

    run_list(qf_ref, kf_ref, sf_ref, nf_ref[b], masked=False)
    run_list(qp_ref, kp_ref, sp_ref, np_ref[b], masked=True)

    lam_v = lam_ref[...]
    lam = (jnp.exp(jnp.sum(lam_v[0:1] * lam_v[1:2], axis=-1, keepdims=True))
           - jnp.exp(jnp.sum(lam_v[2:3] * lam_v[3:4], axis=-1, keepdims=True))
           + lambda_init)
    out_w = subw_ref[...] * (1.0 - lambda_init)

    def finish(qi, carry):
        a1 = acc_ref[0, qi]
        a2 = acc_ref[1, qi]
        o_t = (a1[0:V_DIM] * (1.0 / a1[V_DIM:V_DIM + 1])
               - a2[0:V_DIM] * (lam / a2[V_DIM:V_DIM + 1]))
        o_t = o_t * lax.rsqrt(jnp.mean(o_t * o_t, axis=0, keepdims=True) + NORM_EPS)
        o_ref[pl.ds(pl.multiple_of(qi * TQ, TQ), TQ), :] = (o_t.T * out_w).astype(o_ref.dtype)
        return carry

    lax.fori_loop(0, nq, finish, 0, unroll=2)


def _attn_schedule(positions):
    b, s = positions.shape
    nq, nk = s // TQ, s // TK
    cid = jnp.right_shift(positions, CHUNK_SHIFT)
    qc = cid.reshape(b, nq, TQ)
    kc = cid.reshape(b, nk, TK)
    qmin, qmax = qc.min(-1), qc.max(-1)
    kmin, kmax = kc.min(-1), kc.max(-1)
    need = kmin[:, None, :] <= qmax[:, :, None]
    full = kmax[:, None, :] <= qmin[:, :, None]
    part = need & ~full

    def pair_list(sel):
        sel = sel.reshape(b, nq * nk)
        n = sel.sum(-1).astype(jnp.int32)
        order = jnp.argsort(~sel, axis=-1, stable=True).astype(jnp.int32)
        order = jnp.pad(order, ((0, 0), (0, PAIR_UNROLL)))
        qb = order // nk
        slot = jnp.where(jnp.arange(order.shape[1])[None, :] < n[:, None], qb, nq)
        return qb, order % nk, slot, n

    kcid_lanes = jnp.broadcast_to(cid.reshape(b * s, 1), (b * s, LANES))
    return pair_list(full) + pair_list(part) + (cid.reshape(b, nq, 1, TQ), kcid_lanes)


def _attn(proj, vt, sched, lam_params, subw, layer, lambda_init, b, s):
    qcid, kcid = sched[8:]
    nq, nk = s // TQ, s // TK
    kern = functools.partial(_attn_kernel, lambda_init=lambda_init, nq=nq)
    grid_spec = pltpu.PrefetchScalarGridSpec(
        num_scalar_prefetch=8,
        grid=(b, N_HEADS),
        in_specs=[
            pl.BlockSpec((s, V_DIM), lambda bi, h, *_: (bi, h)),
            pl.BlockSpec((s, V_DIM), lambda bi, h, *_: (bi, N_HEADS + h)),
            pl.BlockSpec((nk, V_DIM, TK), lambda bi, h, *_: (bi, h, 0)),
            pl.BlockSpec((None, nq, 1, TQ), lambda bi, h, *_: (bi, 0, 0, 0)),
            pl.BlockSpec((s, LANES), lambda bi, h, *_: (bi, 0)),
            pl.BlockSpec((None, 4, HEAD_DIM), lambda bi, h, *_: (layer, 0, 0)),
            pl.BlockSpec((None, 1, V_DIM), lambda bi, h, *_: (layer, 0, 0)),
        ],
        out_specs=pl.BlockSpec((s, V_DIM), lambda bi, h, *_: (bi, h)),
        scratch_shapes=[
            pltpu.VMEM((2, TK, TQ), _F32),
            pltpu.VMEM((2, 1, TQ), _F32),
            pltpu.VMEM((2, nq + 1, 1, TQ), _F32),
            pltpu.VMEM((2, nq + 1, VT_ROWS, TQ), _F32),
        ],
    )
    return pl.pallas_call(
        kern,
        out_shape=jax.ShapeDtypeStruct((b * s, N_HEADS * V_DIM), _BF16),
        grid_spec=grid_spec,
        compiler_params=pltpu.CompilerParams(
            dimension_semantics=("arbitrary", "arbitrary"),
            vmem_limit_bytes=VMEM_LIMIT),
        name="attn",
    )(*sched[:8], proj, proj, vt, qcid, kcid, lam_params, subw)


def _mix_kernel(x_ref, o_ref, bg_ref, cg_ref, u_ref, ga_ref, gb_ref, cgh_ref, uh_ref,
                cw_ref, wa_ref, wb_ref, wo_ref, out_ref, ext_ref, *, tiles_per_seq):
    i = pl.program_id(0)
    halo = cgh_ref[...].astype(_F32) * uh_ref[...].astype(_F32)
    halo = jnp.where(i % tiles_per_seq == 0, 0.0, halo)
    ext_ref[0:SUBLANES, :] = halo
    ext_ref[SUBLANES:, :] = cg_ref[...].astype(_F32) * u_ref[...].astype(_F32)
    cw = cw_ref[...]
    conv = (cw[0:1] * ext_ref[SUBLANES - 2:SUBLANES - 2 + TM_MIX, :]
            + cw[1:2] * ext_ref[SUBLANES - 1:SUBLANES - 1 + TM_MIX, :]
            + cw[2:3] * ext_ref[SUBLANES:, :])
    yb_in = (bg_ref[...].astype(_F32) * conv).astype(_BF16)
    y_b = jnp.dot(yb_in, wb_ref[...], preferred_element_type=_F32)
    y_a = jnp.dot(o_ref[...], wa_ref[...], preferred_element_type=_F32)
    mixed = (jax.nn.sigmoid(ga_ref[...].astype(_F32)) * y_a
             + jax.nn.sigmoid(gb_ref[...].astype(_F32)) * y_b)
    out_ref[...] = x_ref[...] + jnp.dot(mixed.astype(_BF16), wo_ref[...], preferred_element_type=_F32)


def _mix(x, attn_o, proj, conv_w, wa, wb, wo, layer, s):
    m = x.shape[0]
    tiles_per_seq = s // TM_MIX
    halo_blocks = TM_MIX // SUBLANES

    def col(c):
        return pl.BlockSpec((TM_MIX, D_MODEL), lambda i: (i, c))

    def halo(c):
        return pl.BlockSpec((SUBLANES, D_MODEL), lambda i: (jnp.maximum(i * halo_blocks - 1, 0), c))

    kern = functools.partial(_mix_kernel, tiles_per_seq=tiles_per_seq)
    return pl.pallas_call(
        kern,
        out_shape=jax.ShapeDtypeStruct((m, D_MODEL), _F32),
        grid=(m // TM_MIX,),
        in_specs=[col(0), col(0), col(2), col(3), col(4), col(5), col(6), halo(3), halo(4),
                  _layer_spec((CONV_K, D_MODEL), layer), _layer_spec((D_MODEL, D_MODEL), layer),
                  _layer_spec((D_MODEL, D_MODEL), layer), _layer_spec((D_MODEL, D_MODEL), layer)],
        out_specs=col(0),
        scratch_shapes=[pltpu.VMEM((TM_MIX + SUBLANES, D_MODEL), _F32)],
        compiler_params=pltpu.CompilerParams(
            dimension_semantics=("arbitrary",), vmem_limit_bytes=VMEM_LIMIT),
        name="mix",
    )(x, attn_o, proj, proj, proj, proj, proj, proj, proj, conv_w, wa, wb, wo)


def _ffn_kernel(x_ref, nw_ref, wg_ref, wu_ref, wd_ref, fw_ref, out_ref, *, final_norm):
    x = x_ref[...]
    hn = _rms(x, nw_ref[...]).astype(_BF16)
    acc = x
    for c in range(D_FF // FF_CHUNK):
        sl = slice(c * FF_CHUNK, (c + 1) * FF_CHUNK)
        g = jnp.dot(hn, wg_ref[:, sl], preferred_element_type=_F32)
        u = jnp.dot(hn, wu_ref[:, sl], preferred_element_type=_F32)
        h = (jax.nn.silu(g) * u).astype(_BF16)
        acc = acc + jnp.dot(h, wd_ref[sl, :], preferred_element_type=_F32)
    if final_norm:
        acc = _rms(acc, fw_ref[...])
    out_ref[...] = acc


def _ffn(x, norm_w, wg, wu, wd, final_w, layer, final_norm):
    m = x.shape[0]
    tile = pl.BlockSpec((TM_FFN, D_MODEL), lambda i: (i, 0))
    kern = functools.partial(_ffn_kernel, final_norm=final_norm)
    return pl.pallas_call(
        kern,
        out_shape=jax.ShapeDtypeStruct((m, D_MODEL), _F32),
        grid=(m // TM_FFN,),
        in_specs=[tile, _layer_spec((1, D_MODEL), layer), _layer_spec((D_MODEL, D_FF), layer),
                  _layer_spec((D_MODEL, D_FF), layer), _layer_spec((D_FF, D_MODEL), layer),
                  pl.BlockSpec((1, D_MODEL), lambda i: (0, 0))],
        out_specs=tile,
        compiler_params=pltpu.CompilerParams(
            dimension_semantics=("arbitrary",), vmem_limit_bytes=VMEM_LIMIT),
        name="ffn",
    )(x, norm_w, wg, wu, wd, final_w.reshape(1, D_MODEL))


def kernel(x, positions, mix_norm, w_in, lambda_q1, lambda_k1, lambda_q2, lambda_k2, subln_w, conv_w,
           w_branch_a, w_branch_b, w_out, ffn_norm, w_gate, w_up, w_down, final_norm):
    b, s, d = x.shape
    assert d == D_MODEL and s % TQ == 0 and s % TM_MIX == 0 and (b * s) % TM_PROJ == 0
    h = x.reshape(b * s, d)
    cos_t, sin_t = _rope_tables(positions)
    sched = _attn_schedule(positions)
    mix_norm3 = mix_norm.reshape(DEPTH, 1, D_MODEL)
    ffn_norm3 = ffn_norm.reshape(DEPTH, 1, D_MODEL)
    subln3 = subln_w.reshape(DEPTH, 1, V_DIM)
    lam_params = jnp.stack([lambda_q1, lambda_k1, lambda_q2, lambda_k2], axis=1).astype(_F32)
    w_in, w_branch_a, w_branch_b, w_out, w_gate, w_up, w_down = (
        w.astype(_BF16) for w in (w_in, w_branch_a, w_branch_b, w_out, w_gate, w_up, w_down))
    for l in range(DEPTH):
        lambda_init = 0.8 - 0.6 * math.exp(-0.3 * l)
        proj, vt = _inproj(h, mix_norm3, w_in, l, cos_t, sin_t)
        attn_o = _attn(proj, vt, sched, lam_params, subln3, l, lambda_init, b, s)
        h = _mix(h, attn_o, proj, conv_w, w_branch_a, w_branch_b, w_out, l, s)
        h = _ffn(h, ffn_norm3, w_gate, w_up, w_down, final_norm, l, l == DEPTH - 1)
    return h.reshape(b, s, d)
```

```python
import functools
import math

import jax
import jax.numpy as jnp
from jax import lax
from jax.experimental import pallas as pl
from jax.experimental.pallas import tpu as pltpu

D_MODEL = 1024
DEPTH = 4
CHUNK_SHIFT = 6
N_HEADS = 8
HEAD_DIM = 64
V_DIM = 2 * HEAD_DIM
ROT_DIM = HEAD_DIM // 4
ROPE_THETA = 500000.0
CONV_K = 3
D_FF = 2816
IN_WIDTH = 8 * D_MODEL
PROJ_WIDTH = 5 * D_MODEL
NORM_EPS = 1e-6
NEG_INF = -1e30
LOG2_E = 1.4426950408889634

LANES = 128
SUBLANES = 8
BF16_ROWS = 16
VMEM_LIMIT = 56 * 1024 * 1024

TM_PROJ = 512
TN_PROJ = 512
TQ = 512
TK = 512
VT_ROWS = V_DIM + BF16_ROWS
PAIR_UNROLL = 4
PAIR_UNROLL_MASKED = 2
TM_MIX = 512
TM_FFN = 512
FF_CHUNK = 1408

_F32 = jnp.float32
_BF16 = jnp.bfloat16


def _rms(x, w):
    return x * lax.rsqrt(jnp.mean(x * x, axis=-1, keepdims=True) + NORM_EPS) * w


def _layer_spec(shape, layer):
    return pl.BlockSpec((None,) + shape, lambda *_: (layer,) + (0,) * len(shape),
                        pipeline_mode=pl.Buffered(1))


def _rope_table_kernel(ang_ref, cos_ref, sin_ref):
    ang = ang_ref[...]
    p = lax.broadcasted_iota(jnp.int32, ang.shape, 1) % HEAD_DIM
    c = jnp.cos(ang)
    s = jnp.sin(ang)
    cos_ref[...] = jnp.where(p < ROT_DIM, c, 1.0)
    sin_ref[...] = jnp.where(p < ROT_DIM // 2, -s, jnp.where(p < ROT_DIM, s, 0.0))


def _rope_tables(positions):
    m = positions.size
    freqs = ROPE_THETA ** (-jnp.arange(0, ROT_DIM, 2, dtype=_F32) / ROT_DIM)
    lane_freq = jnp.tile(freqs, LANES // (ROT_DIM // 2))
    ang = positions.reshape(m, 1).astype(_F32) * lane_freq[None, :]
    tm = 2048
    spec = pl.BlockSpec((tm, LANES), lambda i: (i, 0))
    return pl.pallas_call(
        _rope_table_kernel,
        out_shape=(jax.ShapeDtypeStruct((m, LANES), _F32),) * 2,
        grid=(m // tm,),
        in_specs=[spec],
        out_specs=(spec, spec),
        name="rope_tables",
    )(ang)


def _inproj_kernel(x_ref, nw_ref, w_ref, cos_ref, sin_ref, cw_ref, o_ref, vt_ref, ext_ref, carry_ref,
                   *, tiles_per_seq):
    i = pl.program_id(0)

    @pl.when(i == 0)
    def _():
        carry_ref[...] = jnp.zeros(carry_ref.shape, _F32)

    xn = _rms(x_ref[...], nw_ref[...]).astype(_BF16)

    def proj(lo):
        return jnp.dot(xn, w_ref[:, lo:lo + TN_PROJ], preferred_element_type=_F32)

    cos = cos_ref[...]
    sin = sin_ref[...]
    first_half = (lax.broadcasted_iota(jnp.int32, cos.shape, 1) % HEAD_DIM) < ROT_DIM // 2
    q_scale = HEAD_DIM ** -0.5 * LOG2_E
    for lo in range(0, 2 * D_MODEL, TN_PROJ):
        acc = proj(lo)
        cs, sn = (cos * q_scale, sin * q_scale) if lo < D_MODEL else (cos, sin)
        for c in range(TN_PROJ // LANES):
            t = acc[:, c * LANES:(c + 1) * LANES]
            partner = jnp.where(first_half,
                                pltpu.roll(t, LANES - ROT_DIM // 2, axis=1),
                                pltpu.roll(t, ROT_DIM // 2, axis=1))
            o_ref[:, lo + c * LANES:lo + (c + 1) * LANES] = (t * cs + partner * sn).astype(o_ref.dtype)

    for lo in range(0, D_MODEL, TN_PROJ):
        vt_ref[0, lo:lo + TN_PROJ, :] = proj(2 * D_MODEL + lo).T.astype(vt_ref.dtype)

    seq_start = i % tiles_per_seq == 0
    for n, lo in enumerate(range(0, D_MODEL, TN_PROJ)):
        cols = slice(lo, lo + TN_PROJ)
        u = proj(5 * D_MODEL + lo)
        cu = proj(4 * D_MODEL + lo) * u
        ext_ref[n, 0:SUBLANES, :] = jnp.where(seq_start, 0.0, carry_ref[:, cols])
        ext_ref[n, SUBLANES:, :] = cu
        carry_ref[:, cols] = cu[TM_PROJ - SUBLANES:, :]
        cw = cw_ref[:, cols]
        conv = (cw[0:1] * ext_ref[n, SUBLANES - 2:SUBLANES - 2 + TM_PROJ, :]
                + cw[1:2] * ext_ref[n, SUBLANES - 1:SUBLANES - 1 + TM_PROJ, :]
                + cw[2:3] * ext_ref[n, SUBLANES:, :])
        b_gate = proj(3 * D_MODEL + lo)
        o_ref[:, 2 * D_MODEL + lo:2 * D_MODEL + lo + TN_PROJ] = (b_gate * conv).astype(o_ref.dtype)

    for lo in range(0, 2 * D_MODEL, TN_PROJ):
        o_ref[:, 3 * D_MODEL + lo:3 * D_MODEL + lo + TN_PROJ] = proj(6 * D_MODEL + lo).astype(o_ref.dtype)


def _inproj(x, norm_w, w_bf16, conv_w, layer, cos_t, sin_t, s):
    m = x.shape[0]
    assert TM_PROJ == TK and s % TM_PROJ == 0
    kern = functools.partial(_inproj_kernel, tiles_per_seq=s // TM_PROJ)
    return pl.pallas_call(
        kern,
        out_shape=(jax.ShapeDtypeStruct((m, PROJ_WIDTH), _BF16),
                   jax.ShapeDtypeStruct((m // TK, D_MODEL, TK), _BF16)),
        grid=(m // TM_PROJ,),
        in_specs=[
            pl.BlockSpec((TM_PROJ, D_MODEL), lambda i: (i, 0)),
            _layer_spec((1, D_MODEL), layer),
            _layer_spec((D_MODEL, IN_WIDTH), layer),
            pl.BlockSpec((TM_PROJ, LANES), lambda i: (i, 0)),
            pl.BlockSpec((TM_PROJ, LANES), lambda i: (i, 0)),
            _layer_spec((CONV_K, D_MODEL), layer),
        ],
        out_specs=(pl.BlockSpec((TM_PROJ, PROJ_WIDTH), lambda i: (i, 0)),
                   pl.BlockSpec((1, D_MODEL, TK), lambda i: (i, 0, 0))),
        scratch_shapes=[pltpu.VMEM((D_MODEL // TN_PROJ, TM_PROJ + SUBLANES, TN_PROJ), _F32),
                        pltpu.VMEM((SUBLANES, D_MODEL), _F32)],
        compiler_params=pltpu.CompilerParams(
            dimension_semantics=("arbitrary",), vmem_limit_bytes=VMEM_LIMIT),
        name="inproj",
    )(x, norm_w, w_bf16, cos_t, sin_t, conv_w)


def _attn_kernel(qf_ref, kf_ref, sf_ref, nf_ref, qp_ref, kp_ref, sp_ref, np_ref,
                 q_ref, k_ref, vt_ref, qcid_ref, kcid_ref, lam_ref, subw_ref, o_ref,
                 s_ref, mb_ref, m_ref, acc_ref, *, lambda_init, nq):
    b = pl.program_id(0)

    @pl.when((b == 0) & (pl.program_id(1) == 0))
    def _():
        m_ref[...] = jnp.full(m_ref.shape, NEG_INF, _F32)
        acc_ref[...] = jnp.zeros(acc_ref.shape, _F32)

    lane = lax.broadcasted_iota(jnp.int32, (TQ, V_DIM), 1)
    ones_rows = jnp.ones((VT_ROWS - V_DIM, TK), _BF16)

    def load_pair(qi, ki, masked):
        q = q_ref[pl.ds(pl.multiple_of(qi * TQ, TQ), TQ), :]
        zero = jnp.zeros_like(q)
        qms = (jnp.where(lane < HEAD_DIM, q, zero), jnp.where(lane >= HEAD_DIM, q, zero))
        ks = pl.multiple_of(ki * TK, TK)
        k = k_ref[pl.ds(ks, TK), :]
        allowed = None
        if masked:
            kc = kcid_ref[pl.ds(ks, TK), :]
            allowed = jnp.concatenate([kc] * (TQ // LANES), axis=1) <= qcid_ref[qi]
        return k, qms, allowed

    def scores(j, k, qms, allowed):
        s = lax.dot_general(k, qms[j], (((1,), (1,)), ((), ())), preferred_element_type=_F32)
        if allowed is not None:
            s = jnp.where(allowed, s, NEG_INF)
        s_ref[j] = s
        mb_ref[j] = jnp.max(s, axis=0, keepdims=True)

    def run_list(ql_ref, kl_ref, sl_ref, n, masked, unroll):
        @pl.when(n > 0)
        def _():
            k0, qms0, allowed0 = load_pair(ql_ref[b, 0], kl_ref[b, 0], masked)
            for j in range(2):
                scores(j, k0, qms0, allowed0)

            def body(i, carry):
                for u in range(unroll):
                    t = i * unroll + u
                    slot = sl_ref[b, t]
                    vt = jnp.concatenate([vt_ref[kl_ref[b, t]], ones_rows], axis=0)
                    kn, qmsn, allowedn = load_pair(ql_ref[b, t + 1], kl_ref[b, t + 1], masked)
                    for j in range(2):
                        m_old = m_ref[j, slot]
                        m_new = jnp.maximum(m_old, mb_ref[j])
                        alpha = jnp.exp2(m_old - m_new)
                        p = jnp.exp2(s_ref[j] - m_new).astype(_BF16)
                        scores(j, kn, qmsn, allowedn)
                        acc_ref[j, slot] = (alpha * acc_ref[j, slot]
                                            + jnp.dot(vt, p, preferred_element_type=_F32))
                        m_ref[j, slot] = m_new
                return carry

            lax.fori_loop(0, (n + unroll - 1) // unroll, body, 0)

    run_list(qf_ref, kf_ref, sf_ref, nf_ref[b], masked=False, unroll=PAIR_UNROLL)
    run_list(qp_ref, kp_ref, sp_ref, np_ref[b], masked=True, unroll=PAIR_UNROLL_MASKED)

    lam_v = lam_ref[...]
    lam = (jnp.exp(jnp.sum(lam_v[0:1] * lam_v[1:2], axis=-1, keepdims=True))
           - jnp.exp(jnp.sum(lam_v[2:3] * lam_v[3:4], axis=-1, keepdims=True))
           + lambda_init)
    out_w = subw_ref[...] * (1.0 - lambda_init)

    def finish(qi, carry):
        a1 = acc_ref[0, qi]
        a2 = acc_ref[1, qi]
        o_t = (a1[0:V_DIM] * (1.0 / a1[V_DIM:V_DIM + 1])
               - a2[0:V_DIM] * (lam / a2[V_DIM:V_DIM + 1]))
        o_t = o_t * lax.rsqrt(jnp.mean(o_t * o_t, axis=0, keepdims=True) + NORM_EPS)
        o_ref[pl.ds(pl.multiple_of(qi * TQ, TQ), TQ), :] = (o_t.T * out_w).astype(o_ref.dtype)
        reset(qi)
        return carry

    def reset(slot):
        for j in range(2):
            m_ref[j, slot] = jnp.full((1, TQ), NEG_INF, _F32)
            acc_ref[j, slot] = jnp.zeros((VT_ROWS, TQ), _F32)

    lax.fori_loop(0, nq, finish, 0, unroll=4)
    reset(nq)


def _attn_schedule(positions):
    b, s = positions.shape
    nq, nk = s // TQ, s // TK
    cid = jnp.right_shift(positions, CHUNK_SHIFT)
    qc = cid.reshape(b, nq, TQ)
    kc = cid.reshape(b, nk, TK)
    qmin, qmax = qc.min(-1), qc.max(-1)
    kmin, kmax = kc.min(-1), kc.max(-1)
    need = kmin[:, None, :] <= qmax[:, :, None]
    full = kmax[:, None, :] <= qmin[:, :, None]
    part = need & ~full

    def pair_list(sel):
        sel = sel.reshape(b, nq * nk)
        n = sel.sum(-1).astype(jnp.int32)
        order = jnp.argsort(~sel, axis=-1, stable=True).astype(jnp.int32)
        order = jnp.pad(order, ((0, 0), (0, PAIR_UNROLL)))
        qb = order // nk
        slot = jnp.where(jnp.arange(order.shape[1])[None, :] < n[:, None], qb, nq)
        return qb, order % nk, slot, n

    kcid_lanes = jnp.broadcast_to(cid.reshape(b * s, 1), (b * s, LANES))
    return pair_list(full) + pair_list(part) + (cid.reshape(b, nq, 1, TQ), kcid_lanes)


def _attn(proj, vt, sched, lam_params, subw, layer, lambda_init, b, s):
    qcid, kcid = sched[8:]
    nq, nk = s // TQ, s // TK
    kern = functools.partial(_attn_kernel, lambda_init=lambda_init, nq=nq)
    grid_spec = pltpu.PrefetchScalarGridSpec(
        num_scalar_prefetch=8,
        grid=(b, N_HEADS),
        in_specs=[
            pl.BlockSpec((s, V_DIM), lambda bi, h, *_: (bi, h)),
            pl.BlockSpec((s, V_DIM), lambda bi, h, *_: (bi, N_HEADS + h)),
            pl.BlockSpec((nk, V_DIM, TK), lambda bi, h, *_: (bi, h, 0)),
            pl.BlockSpec((None, nq, 1, TQ), lambda bi, h, *_: (bi, 0, 0, 0)),
            pl.BlockSpec((s, LANES), lambda bi, h, *_: (bi, 0)),
            pl.BlockSpec((None, 4, HEAD_DIM), lambda bi, h, *_: (layer, 0, 0)),
            pl.BlockSpec((None, 1, V_DIM), lambda bi, h, *_: (layer, 0, 0)),
        ],
        out_specs=pl.BlockSpec((s, V_DIM), lambda bi, h, *_: (bi, h)),
        scratch_shapes=[
            pltpu.VMEM((2, TK, TQ), _F32),
            pltpu.VMEM((2, 1, TQ), _F32),
            pltpu.VMEM((2, nq + 1, 1, TQ), _F32),
            pltpu.VMEM((2, nq + 1, VT_ROWS, TQ), _F32),
        ],
    )
    return pl.pallas_call(
        kern,
        out_shape=jax.ShapeDtypeStruct((b * s, N_HEADS * V_DIM), _BF16),
        grid_spec=grid_spec,
        compiler_params=pltpu.CompilerParams(
            dimension_semantics=("arbitrary", "arbitrary"),
            vmem_limit_bytes=VMEM_LIMIT),
        name="attn",
    )(*sched[:8], proj, proj, vt, qcid, kcid, lam_params, subw)


def _mix_kernel(x_ref, o_ref, yb_ref, ga_ref, gb_ref, wa_ref, wb_ref, wo_ref, out_ref):
    y_b = jnp.dot(yb_ref[...], wb_ref[...], preferred_element_type=_F32)
    y_a = jnp.dot(o_ref[...], wa_ref[...], preferred_element_type=_F32)
    mixed = (jax.nn.sigmoid(ga_ref[...].astype(_F32)) * y_a
             + jax.nn.sigmoid(gb_ref[...].astype(_F32)) * y_b)
    out_ref[...] = x_ref[...] + jnp.dot(mixed.astype(_BF16), wo_ref[...], preferred_element_type=_F32)


def _mix(x, attn_o, proj, wa, wb, wo, layer):
    m = x.shape[0]

    def col(c):
        return pl.BlockSpec((TM_MIX, D_MODEL), lambda i: (i, c))

    return pl.pallas_call(
        _mix_kernel,
        out_shape=jax.ShapeDtypeStruct((m, D_MODEL), _F32),
        grid=(m // TM_MIX,),
        in_specs=[col(0), col(0), col(2), col(3), col(4), _layer_spec((D_MODEL, D_MODEL), layer),
                  _layer_spec((D_MODEL, D_MODEL), layer), _layer_spec((D_MODEL, D_MODEL), layer)],
        out_specs=col(0),
        compiler_params=pltpu.CompilerParams(
            dimension_semantics=("arbitrary",), vmem_limit_bytes=VMEM_LIMIT),
        name="mix",
    )(x, attn_o, proj, proj, proj, wa, wb, wo)


def _ffn_kernel(x_ref, nw_ref, wg_ref, wu_ref, wd_ref, fw_ref, out_ref, *, final_norm):
    x = x_ref[...]
    hn = _rms(x, nw_ref[...]).astype(_BF16)
    acc = x
    for c in range(D_FF // FF_CHUNK):
        sl = slice(c * FF_CHUNK, (c + 1) * FF_CHUNK)
        g = jnp.dot(hn, wg_ref[:, sl], preferred_element_type=_F32)
        u = jnp.dot(hn, wu_ref[:, sl], preferred_element_type=_F32)
        h = (jax.nn.silu(g) * u).astype(_BF16)
        acc = acc + jnp.dot(h, wd_ref[sl, :], preferred_element_type=_F32)
    if final_norm:
        acc = _rms(acc, fw_ref[...])
    out_ref[...] = acc


def _ffn(x, norm_w, wg, wu, wd, final_w, layer, final_norm):
    m = x.shape[0]
    tile = pl.BlockSpec((TM_FFN, D_MODEL), lambda i: (i, 0))
    kern = functools.partial(_ffn_kernel, final_norm=final_norm)
    return pl.pallas_call(
        kern,
        out_shape=jax.ShapeDtypeStruct((m, D_MODEL), _F32),
        grid=(m // TM_FFN,),
        in_specs=[tile, _layer_spec((1, D_MODEL), layer), _layer_spec((D_MODEL, D_FF), layer),
                  _layer_spec((D_MODEL, D_FF), layer), _layer_spec((D_FF, D_MODEL), layer),
                  pl.BlockSpec((1, D_MODEL), lambda i: (0, 0))],
        out_specs=tile,
        compiler_params=pltpu.CompilerParams(
            dimension_semantics=("arbitrary",), vmem_limit_bytes=VMEM_LIMIT),
        name="ffn",
    )(x, norm_w, wg, wu, wd, final_w.reshape(1, D_MODEL))


def kernel(x, positions, mix_norm, w_in, lambda_q1, lambda_k1, lambda_q2, lambda_k2, subln_w, conv_w,
           w_branch_a, w_branch_b, w_out, ffn_norm, w_gate, w_up, w_down, final_norm):
    b, s, d = x.shape
    assert d == D_MODEL and s % TQ == 0 and s % TM_MIX == 0 and (b * s) % TM_PROJ == 0
    h = x.reshape(b * s, d)
    cos_t, sin_t = _rope_tables(positions)
    sched = _attn_schedule(positions)
    mix_norm3 = mix_norm.reshape(DEPTH, 1, D_MODEL)
    ffn_norm3 = ffn_norm.reshape(DEPTH, 1, D_MODEL)
    subln3 = subln_w.reshape(DEPTH, 1, V_DIM)
    lam_params = jnp.stack([lambda_q1, lambda_k1, lambda_q2, lambda_k2], axis=1).astype(_F32)
    w_in, w_branch_a, w_branch_b, w_out, w_gate, w_up, w_down = (
        w.astype(_BF16) for w in (w_in, w_branch_a, w_branch_b, w_out, w_gate, w_up, w_down))
    for l in range(DEPTH):
        lambda_init = 0.8 - 0.6 * math.exp(-0.3 * l)
        proj, vt = _inproj(h, mix_norm3, w_in, conv_w, l, cos_t, sin_t, s)
        attn_o = _attn(proj, vt, sched, lam_params, subln3, l, lambda_init, b, s)
        h = _mix(h, attn_o, proj, w_branch_a, w_branch_b, w_out, l)
        h = _ffn(h, ffn_norm3, w_gate, w_up, w_down, final_norm, l, l == DEPTH - 1)
    return h.reshape(b, s, d)
```

```python
import functools
import math

import jax
import jax.numpy as jnp
from jax import lax
from jax.experimental import pallas as pl
from jax.experimental.pallas import tpu as pltpu

D_MODEL = 1024
DEPTH = 4
CHUNK_SHIFT = 6
N_HEADS = 8
HEAD_DIM = 64
V_DIM = 2 * HEAD_DIM
ROT_DIM = HEAD_DIM // 4
ROPE_THETA = 500000.0
CONV_K = 3
D_FF = 2816
IN_WIDTH = 8 * D_MODEL
PROJ_WIDTH = 5 * D_MODEL
NORM_EPS = 1e-6
NEG_INF = -1e30
LOG2_E = 1.4426950408889634

LANES = 128
SUBLANES = 8
BF16_ROWS = 16
VMEM_LIMIT = 56 * 1024 * 1024

TM_PROJ = 512
TN_PROJ = 512
TQ = 512
TK = 512
VT_ROWS = V_DIM + BF16_ROWS
PAIR_UNROLL = 7
PAIR_UNROLL_MASKED = 4
TM_MIX = 512
TM_FFN = 512
MXU_DIM = 256
FF_SPLITS = (0, 6 * MXU_DIM, D_FF)

_F32 = jnp.float32
_BF16 = jnp.bfloat16


def _rms(x, w):
    return x * lax.rsqrt(jnp.mean(x * x, axis=-1, keepdims=True) + NORM_EPS) * w


def _layer_spec(shape, layer):
    return pl.BlockSpec((None,) + shape, lambda *_: (layer,) + (0,) * len(shape),
                        pipeline_mode=pl.Buffered(1))


def _rope_table_kernel(ang_ref, cos_ref, sin_ref):
    ang = ang_ref[...]
    p = lax.broadcasted_iota(jnp.int32, ang.shape, 1) % HEAD_DIM
    c = jnp.cos(ang)
    s = jnp.sin(ang)
    cos_ref[...] = jnp.where(p < ROT_DIM, c, 1.0)
    sin_ref[...] = jnp.where(p < ROT_DIM // 2, -s, jnp.where(p < ROT_DIM, s, 0.0))


def _rope_tables(positions):
    m = positions.size
    freqs = ROPE_THETA ** (-jnp.arange(0, ROT_DIM, 2, dtype=_F32) / ROT_DIM)
    lane_freq = jnp.tile(freqs, LANES // (ROT_DIM // 2))
    ang = positions.reshape(m, 1).astype(_F32) * lane_freq[None, :]
    tm = 2048
    spec = pl.BlockSpec((tm, LANES), lambda i: (i, 0))
    return pl.pallas_call(
        _rope_table_kernel,
        out_shape=(jax.ShapeDtypeStruct((m, LANES), _F32),) * 2,
        grid=(m // tm,),
        in_specs=[spec],
        out_specs=(spec, spec),
        name="rope_tables",
    )(ang)


def _inproj_kernel(x_ref, nw_ref, w_ref, cos_ref, sin_ref, cw_ref, o_ref, vt_ref, ext_ref, carry_ref,
                   *, tiles_per_seq):
    i = pl.program_id(0)

    @pl.when(i == 0)
    def _():
        carry_ref[...] = jnp.zeros(carry_ref.shape, _F32)

    xn = _rms(x_ref[...], nw_ref[...]).astype(_BF16)

    def proj(lo):
        return jnp.dot(xn, w_ref[:, lo:lo + TN_PROJ], preferred_element_type=_F32)

    cos = cos_ref[...]
    sin = sin_ref[...]
    first_half = (lax.broadcasted_iota(jnp.int32, cos.shape, 1) % HEAD_DIM) < ROT_DIM // 2
    q_scale = HEAD_DIM ** -0.5 * LOG2_E
    for lo in range(0, 2 * D_MODEL, TN_PROJ):
        acc = proj(lo)
        cs, sn = (cos * q_scale, sin * q_scale) if lo < D_MODEL else (cos, sin)
        for c in range(TN_PROJ // LANES):
            t = acc[:, c * LANES:(c + 1) * LANES]
            partner = jnp.where(first_half,
                                pltpu.roll(t, LANES - ROT_DIM // 2, axis=1),
                                pltpu.roll(t, ROT_DIM // 2, axis=1))
            o_ref[:, lo + c * LANES:lo + (c + 1) * LANES] = (t * cs + partner * sn).astype(o_ref.dtype)

    for lo in range(0, D_MODEL, TN_PROJ):
        vt_ref[0, lo:lo + TN_PROJ, :] = proj(2 * D_MODEL + lo).T.astype(vt_ref.dtype)

    seq_start = i % tiles_per_seq == 0
    for n, lo in enumerate(range(0, D_MODEL, TN_PROJ)):
        cols = slice(lo, lo + TN_PROJ)
        u = proj(5 * D_MODEL + lo)
        cu = proj(4 * D_MODEL + lo) * u
        ext_ref[n, 0:SUBLANES, :] = jnp.where(seq_start, 0.0, carry_ref[:, cols])
        ext_ref[n, SUBLANES:, :] = cu
        carry_ref[:, cols] = cu[TM_PROJ - SUBLANES:, :]
        cw = cw_ref[:, cols]
        conv = (cw[0:1] * ext_ref[n, SUBLANES - 2:SUBLANES - 2 + TM_PROJ, :]
                + cw[1:2] * ext_ref[n, SUBLANES - 1:SUBLANES - 1 + TM_PROJ, :]
                + cw[2:3] * ext_ref[n, SUBLANES:, :])
        b_gate = proj(3 * D_MODEL + lo)
        o_ref[:, 2 * D_MODEL + lo:2 * D_MODEL + lo + TN_PROJ] = (b_gate * conv).astype(o_ref.dtype)

    for lo in range(0, 2 * D_MODEL, TN_PROJ):
        o_ref[:, 3 * D_MODEL + lo:3 * D_MODEL + lo + TN_PROJ] = proj(6 * D_MODEL + lo).astype(o_ref.dtype)


def _inproj(x, norm_w, w_bf16, conv_w, layer, cos_t, sin_t, s):
    m = x.shape[0]
    assert TM_PROJ == TK and s % TM_PROJ == 0
    kern = functools.partial(_inproj_kernel, tiles_per_seq=s // TM_PROJ)
    return pl.pallas_call(
        kern,
        out_shape=(jax.ShapeDtypeStruct((m, PROJ_WIDTH), _BF16),
                   jax.ShapeDtypeStruct((m // TK, D_MODEL, TK), _BF16)),
        grid=(m // TM_PROJ,),
        in_specs=[
            pl.BlockSpec((TM_PROJ, D_MODEL), lambda i: (i, 0)),
            _layer_spec((1, D_MODEL), layer),
            _layer_spec((D_MODEL, IN_WIDTH), layer),
            pl.BlockSpec((TM_PROJ, LANES), lambda i: (i, 0)),
            pl.BlockSpec((TM_PROJ, LANES), lambda i: (i, 0)),
            _layer_spec((CONV_K, D_MODEL), layer),
        ],
        out_specs=(pl.BlockSpec((TM_PROJ, PROJ_WIDTH), lambda i: (i, 0)),
                   pl.BlockSpec((1, D_MODEL, TK), lambda i: (i, 0, 0))),
        scratch_shapes=[pltpu.VMEM((D_MODEL // TN_PROJ, TM_PROJ + SUBLANES, TN_PROJ), _F32),
                        pltpu.VMEM((SUBLANES, D_MODEL), _F32)],
        compiler_params=pltpu.CompilerParams(
            dimension_semantics=("arbitrary",), vmem_limit_bytes=VMEM_LIMIT),
        name="inproj",
    )(x, norm_w, w_bf16, cos_t, sin_t, conv_w)


def _attn_kernel(qf_ref, kf_ref, sf_ref, nf_ref, qp_ref, kp_ref, sp_ref, np_ref,
                 q_ref, k_ref, vt_ref, qcid_ref, kcid_ref, lam_ref, subw_ref, o_ref,
                 s_ref, mb_ref, m_ref, acc_ref, *, lambda_init, nq):
    b = pl.program_id(0)

    @pl.when((b == 0) & (pl.program_id(1) == 0))
    def _():
        m_ref[...] = jnp.full(m_ref.shape, NEG_INF, _F32)
        acc_ref[...] = jnp.zeros(acc_ref.shape, _F32)

    lane = lax.broadcasted_iota(jnp.int32, (TQ, V_DIM), 1)
    ones_rows = jnp.ones((VT_ROWS - V_DIM, TK), _BF16)

    def load_pair(qi, ki, masked):
        q = q_ref[pl.ds(pl.multiple_of(qi * TQ, TQ), TQ), :]
        zero = jnp.zeros_like(q)
        qms = (jnp.where(lane < HEAD_DIM, q, zero), jnp.where(lane >= HEAD_DIM, q, zero))
        ks = pl.multiple_of(ki * TK, TK)
        k = k_ref[pl.ds(ks, TK), :]
        allowed = None
        if masked:
            kc = kcid_ref[pl.ds(ks, TK), :]
            allowed = jnp.concatenate([kc] * (TQ // LANES), axis=1) <= qcid_ref[qi]
        return k, qms, allowed

    def scores(j, k, qms, allowed):
        s = lax.dot_general(k, qms[j], (((1,), (1,)), ((), ())), preferred_element_type=_F32)
        if allowed is not None:
            s = jnp.where(allowed, s, NEG_INF)
        s_ref[j] = s
        mb_ref[j] = jnp.max(s, axis=0, keepdims=True)

    def run_list(ql_ref, kl_ref, sl_ref, n, masked, unroll):
        @pl.when(n > 0)
        def _():
            k0, qms0, allowed0 = load_pair(ql_ref[b, 0], kl_ref[b, 0], masked)
            for j in range(2):
                scores(j, k0, qms0, allowed0)

            def body(i, carry):
                for u in range(unroll):
                    t = i * unroll + u
                    slot = sl_ref[b, t]
                    vt = jnp.concatenate([vt_ref[kl_ref[b, t]], ones_rows], axis=0)
                    kn, qmsn, allowedn = load_pair(ql_ref[b, t + 1], kl_ref[b, t + 1], masked)
                    for j in range(2):
                        m_old = m_ref[j, slot]
                        m_new = jnp.maximum(m_old, mb_ref[j])
                        alpha = jnp.exp2(m_old - m_new)
                        p = jnp.exp2(s_ref[j] - m_new).astype(_BF16)
                        scores(j, kn, qmsn, allowedn)
                        acc_ref[j, slot] = (alpha * acc_ref[j, slot]
                                            + jnp.dot(vt, p, preferred_element_type=_F32))
                        m_ref[j, slot] = m_new
                return carry

            lax.fori_loop(0, (n + unroll - 1) // unroll, body, 0)

    run_list(qf_ref, kf_ref, sf_ref, nf_ref[b], masked=False, unroll=PAIR_UNROLL)
    run_list(qp_ref, kp_ref, sp_ref, np_ref[b], masked=True, unroll=PAIR_UNROLL_MASKED)

    lam_v = lam_ref[...]
    lam = (jnp.exp(jnp.sum(lam_v[0:1] * lam_v[1:2], axis=-1, keepdims=True))
           - jnp.exp(jnp.sum(lam_v[2:3] * lam_v[3:4], axis=-1, keepdims=True))
           + lambda_init)
    out_w = subw_ref[...] * (1.0 - lambda_init)

    def reset(slot):
        for j in range(2):
            m_ref[j, slot] = jnp.full((1, TQ), NEG_INF, _F32)
            acc_ref[j, slot] = jnp.zeros((VT_ROWS, TQ), _F32)

    def finish(qi, carry):
        a1 = acc_ref[0, qi]
        a2 = acc_ref[1, qi]
        o_t = (a1[0:V_DIM] * (1.0 / a1[V_DIM:V_DIM + 1])
               - a2[0:V_DIM] * (lam / a2[V_DIM:V_DIM + 1]))
        o_t = o_t * lax.rsqrt(jnp.mean(o_t * o_t, axis=0, keepdims=True) + NORM_EPS)
        o_ref[pl.ds(pl.multiple_of(qi * TQ, TQ), TQ), :] = (o_t.T * out_w).astype(o_ref.dtype)
        reset(qi)
        return carry

    lax.fori_loop(0, nq, finish, 0, unroll=4)
    reset(nq)


def _attn_schedule(positions):
    b, s = positions.shape
    nq, nk = s // TQ, s // TK
    cid = jnp.right_shift(positions, CHUNK_SHIFT)
    qc = cid.reshape(b, nq, TQ)
    kc = cid.reshape(b, nk, TK)
    qmin, qmax = qc.min(-1), qc.max(-1)
    kmin, kmax = kc.min(-1), kc.max(-1)
    need = kmin[:, None, :] <= qmax[:, :, None]
    full = kmax[:, None, :] <= qmin[:, :, None]
    part = need & ~full

    def pair_list(sel):
        sel = sel.reshape(b, nq * nk)
        n = sel.sum(-1).astype(jnp.int32)
        order = jnp.argsort(~sel, axis=-1, stable=True).astype(jnp.int32)
        order = jnp.pad(order, ((0, 0), (0, PAIR_UNROLL)))
        qb = order // nk
        slot = jnp.where(jnp.arange(order.shape[1])[None, :] < n[:, None], qb, nq)
        return qb, order % nk, slot, n

    kcid_lanes = jnp.broadcast_to(cid.reshape(b * s, 1), (b * s, LANES))
    return pair_list(full) + pair_list(part) + (cid.reshape(b, nq, 1, TQ), kcid_lanes)


def _attn(proj, vt, sched, lam_params, subw, layer, lambda_init, b, s):
    qcid, kcid = sched[8:]
    nq, nk = s // TQ, s // TK
    kern = functools.partial(_attn_kernel, lambda_init=lambda_init, nq=nq)
    grid_spec = pltpu.PrefetchScalarGridSpec(
        num_scalar_prefetch=8,
        grid=(b, N_HEADS),
        in_specs=[
            pl.BlockSpec((s, V_DIM), lambda bi, h, *_: (bi, h)),
            pl.BlockSpec((s, V_DIM), lambda bi, h, *_: (bi, N_HEADS + h)),
            pl.BlockSpec((nk, V_DIM, TK), lambda bi, h, *_: (bi, h, 0)),
            pl.BlockSpec((None, nq, 1, TQ), lambda bi, h, *_: (bi, 0, 0, 0)),
            pl.BlockSpec((s, LANES), lambda bi, h, *_: (bi, 0)),
            pl.BlockSpec((None, 4, HEAD_DIM), lambda bi, h, *_: (layer, 0, 0)),
            pl.BlockSpec((None, 1, V_DIM), lambda bi, h, *_: (layer, 0, 0)),
        ],
        out_specs=pl.BlockSpec((s, V_DIM), lambda bi, h, *_: (bi, h)),
        scratch_shapes=[
            pltpu.VMEM((2, TK, TQ), _F32),
            pltpu.VMEM((2, 1, TQ), _F32),
            pltpu.VMEM((2, nq + 1, 1, TQ), _F32),
            pltpu.VMEM((2, nq + 1, VT_ROWS, TQ), _F32),
        ],
    )
    return pl.pallas_call(
        kern,
        out_shape=jax.ShapeDtypeStruct((b * s, N_HEADS * V_DIM), _BF16),
        grid_spec=grid_spec,
        compiler_params=pltpu.CompilerParams(
            dimension_semantics=("arbitrary", "arbitrary"),
            vmem_limit_bytes=VMEM_LIMIT),
        name="attn",
    )(*sched[:8], proj, proj, vt, qcid, kcid, lam_params, subw)


def _mix_kernel(x_ref, o_ref, yb_ref, ga_ref, gb_ref, wa_ref, wb_ref, wo_ref, out_ref):
    y_b = jnp.dot(yb_ref[...], wb_ref[...], preferred_element_type=_F32)
    y_a = jnp.dot(o_ref[...], wa_ref[...], preferred_element_type=_F32)
    mixed = (jax.nn.sigmoid(ga_ref[...].astype(_F32)) * y_a
             + jax.nn.sigmoid(gb_ref[...].astype(_F32)) * y_b)
    out_ref[...] = x_ref[...] + jnp.dot(mixed.astype(_BF16), wo_ref[...], preferred_element_type=_F32)


def _mix(x, attn_o, proj, wa, wb, wo, layer):
    m = x.shape[0]

    def col(c):
        return pl.BlockSpec((TM_MIX, D_MODEL), lambda i: (i, c))

    return pl.pallas_call(
        _mix_kernel,
        out_shape=jax.ShapeDtypeStruct((m, D_MODEL), _F32),
        grid=(m // TM_MIX,),
        in_specs=[col(0), col(0), col(2), col(3), col(4), _layer_spec((D_MODEL, D_MODEL), layer),
                  _layer_spec((D_MODEL, D_MODEL), layer), _layer_spec((D_MODEL, D_MODEL), layer)],
        out_specs=col(0),
        compiler_params=pltpu.CompilerParams(
            dimension_semantics=("arbitrary",), vmem_limit_bytes=VMEM_LIMIT),
        name="mix",
    )(x, attn_o, proj, proj, proj, wa, wb, wo)


def _ffn_kernel(x_ref, nw_ref, wg_ref, wu_ref, wd_ref, fw_ref, out_ref, *, final_norm):
    x = x_ref[...]
    hn = _rms(x, nw_ref[...]).astype(_BF16)
    acc = x
    for lo, hi in zip(FF_SPLITS[:-1], FF_SPLITS[1:]):
        sl = slice(lo, hi)
        g = jnp.dot(hn, wg_ref[:, sl], preferred_element_type=_F32)
        u = jnp.dot(hn, wu_ref[:, sl], preferred_element_type=_F32)
        h = (jax.nn.silu(g) * u).astype(_BF16)
        acc = acc + jnp.dot(h, wd_ref[sl, :], preferred_element_type=_F32)
    if final_norm:
        acc = _rms(acc, fw_ref[...])
    out_ref[...] = acc


def _ffn(x, norm_w, wg, wu, wd, final_w, layer, final_norm):
    m = x.shape[0]
    tile = pl.BlockSpec((TM_FFN, D_MODEL), lambda i: (i, 0))
    kern = functools.partial(_ffn_kernel, final_norm=final_norm)
    return pl.pallas_call(
        kern,
        out_shape=jax.ShapeDtypeStruct((m, D_MODEL), _F32),
        grid=(m // TM_FFN,),
        in_specs=[tile, _layer_spec((1, D_MODEL), layer), _layer_spec((D_MODEL, D_FF), layer),
                  _layer_spec((D_MODEL, D_FF), layer), _layer_spec((D_FF, D_MODEL), layer),
                  pl.BlockSpec((1, D_MODEL), lambda i: (0, 0))],
        out_specs=tile,
        compiler_params=pltpu.CompilerParams(
            dimension_semantics=("arbitrary",), vmem_limit_bytes=VMEM_LIMIT),
        name="ffn",
    )(x, norm_w, wg, wu, wd, final_w.reshape(1, D_MODEL))


def kernel(x, positions, mix_norm, w_in, lambda_q1, lambda_k1, lambda_q2, lambda_k2, subln_w, conv_w,
           w_branch_a, w_branch_b, w_out, ffn_norm, w_gate, w_up, w_down, final_norm):
    b, s, d = x.shape
    assert d == D_MODEL and s % TQ == 0 and s % TM_MIX == 0 and (b * s) % TM_PROJ == 0
    h = x.reshape(b * s, d)
    cos_t, sin_t = _rope_tables(positions)
    sched = _attn_schedule(positions)
    mix_norm3 = mix_norm.reshape(DEPTH, 1, D_MODEL)
    ffn_norm3 = ffn_norm.reshape(DEPTH, 1, D_MODEL)
    subln3 = subln_w.reshape(DEPTH, 1, V_DIM)
    lam_params = jnp.stack([lambda_q1, lambda_k1, lambda_q2, lambda_k2], axis=1).astype(_F32)
    w_in, w_branch_a, w_branch_b, w_out, w_gate, w_up, w_down = (
        w.astype(_BF16) for w in (w_in, w_branch_a, w_branch_b, w_out, w_gate, w_up, w_down))
    for l in range(DEPTH):
        lambda_init = 0.8 - 0.6 * math.exp(-0.3 * l)
        proj, vt = _inproj(h, mix_norm3, w_in, conv_w, l, cos_t, sin_t, s)
        attn_o = _attn(proj, vt, sched, lam_params, subln3, l, lambda_init, b, s)
        h = _mix(h, attn_o, proj, w_branch_a, w_branch_b, w_out, l)
        h = _ffn(h, ffn_norm3, w_gate, w_up, w_down, final_norm, l, l == DEPTH - 1)
    return h.reshape(b, s, d)
```

```python
import functools
import math

import jax
import jax.numpy as jnp
from jax import lax
from jax.experimental import pallas as pl
from jax.experimental.pallas import tpu as pltpu

D_MODEL = 1024
DEPTH = 4
CHUNK_SHIFT = 6
N_HEADS = 8
HEAD_DIM = 64
V_DIM = 2 * HEAD_DIM
ROT_DIM = HEAD_DIM // 4
ROPE_THETA = 500000.0
CONV_K = 3
D_FF = 2816
IN_WIDTH = 8 * D_MODEL
PROJ_WIDTH = 5 * D_MODEL
NORM_EPS = 1e-6
NEG_INF = -1e30
LOG2_E = 1.4426950408889634

LANES = 128
SUBLANES = 8
BF16_ROWS = 16
VMEM_LIMIT = 56 * 1024 * 1024

TM_PROJ = 512
TN_PROJ = 512
TQ = 512
TK = 512
VT_ROWS = V_DIM + BF16_ROWS
PAIR_UNROLL = 14
PAIR_UNROLL_MASKED = 8
TM_MIX = 512
TM_FFN = 512
MXU_DIM = 256
FF_SPLITS = (0, 6 * MXU_DIM, D_FF)

_F32 = jnp.float32
_BF16 = jnp.bfloat16


def _rms(x, w):
    return x * lax.rsqrt(jnp.mean(x * x, axis=-1, keepdims=True) + NORM_EPS) * w


def _layer_spec(shape, layer):
    return pl.BlockSpec((None,) + shape, lambda *_: (layer,) + (0,) * len(shape),
                        pipeline_mode=pl.Buffered(1))


def _rope_table_kernel(ang_ref, cos_ref, sin_ref):
    ang = ang_ref[...]
    p = lax.broadcasted_iota(jnp.int32, ang.shape, 1) % HEAD_DIM
    c = jnp.cos(ang)
    s = jnp.sin(ang)
    cos_ref[...] = jnp.where(p < ROT_DIM, c, 1.0)
    sin_ref[...] = jnp.where(p < ROT_DIM // 2, -s, jnp.where(p < ROT_DIM, s, 0.0))


def _rope_tables(positions):
    m = positions.size
    freqs = ROPE_THETA ** (-jnp.arange(0, ROT_DIM, 2, dtype=_F32) / ROT_DIM)
    lane_freq = jnp.tile(freqs, LANES // (ROT_DIM // 2))
    ang = positions.reshape(m, 1).astype(_F32) * lane_freq[None, :]
    tm = 2048
    spec = pl.BlockSpec((tm, LANES), lambda i: (i, 0))
    return pl.pallas_call(
        _rope_table_kernel,
        out_shape=(jax.ShapeDtypeStruct((m, LANES), _F32),) * 2,
        grid=(m // tm,),
        in_specs=[spec],
        out_specs=(spec, spec),
        name="rope_tables",
    )(ang)


def _inproj_kernel(x_ref, nw_ref, w_ref, cos_ref, sin_ref, cw_ref, o_ref, vt_ref, ext_ref, carry_ref,
                   *, tiles_per_seq):
    i = pl.program_id(0)

    @pl.when(i == 0)
    def _():
        carry_ref[...] = jnp.zeros(carry_ref.shape, _F32)

    xn = _rms(x_ref[...], nw_ref[...]).astype(_BF16)

    def proj(lo):
        return jnp.dot(xn, w_ref[:, lo:lo + TN_PROJ], preferred_element_type=_F32)

    cos = cos_ref[...]
    sin = sin_ref[...]
    first_half = (lax.broadcasted_iota(jnp.int32, cos.shape, 1) % HEAD_DIM) < ROT_DIM // 2
    q_scale = HEAD_DIM ** -0.5 * LOG2_E
    for lo in range(0, 2 * D_MODEL, TN_PROJ):
        acc = proj(lo)
        cs, sn = (cos * q_scale, sin * q_scale) if lo < D_MODEL else (cos, sin)
        for c in range(TN_PROJ // LANES):
            t = acc[:, c * LANES:(c + 1) * LANES]
            partner = jnp.where(first_half,
                                pltpu.roll(t, LANES - ROT_DIM // 2, axis=1),
                                pltpu.roll(t, ROT_DIM // 2, axis=1))
            o_ref[:, lo + c * LANES:lo + (c + 1) * LANES] = (t * cs + partner * sn).astype(o_ref.dtype)

    for lo in range(0, D_MODEL, TN_PROJ):
        vt_ref[0, lo:lo + TN_PROJ, :] = proj(2 * D_MODEL + lo).T.astype(vt_ref.dtype)

    seq_start = i % tiles_per_seq == 0
    for n, lo in enumerate(range(0, D_MODEL, TN_PROJ)):
        cols = slice(lo, lo + TN_PROJ)
        u = proj(5 * D_MODEL + lo)
        cu = proj(4 * D_MODEL + lo) * u
        ext_ref[n, 0:SUBLANES, :] = jnp.where(seq_start, 0.0, carry_ref[:, cols])
        ext_ref[n, SUBLANES:, :] = cu
        carry_ref[:, cols] = cu[TM_PROJ - SUBLANES:, :]
        cw = cw_ref[:, cols]
        conv = (cw[0:1] * ext_ref[n, SUBLANES - 2:SUBLANES - 2 + TM_PROJ, :]
                + cw[1:2] * ext_ref[n, SUBLANES - 1:SUBLANES - 1 + TM_PROJ, :]
                + cw[2:3] * ext_ref[n, SUBLANES:, :])
        b_gate = proj(3 * D_MODEL + lo)
        o_ref[:, 2 * D_MODEL + lo:2 * D_MODEL + lo + TN_PROJ] = (b_gate * conv).astype(o_ref.dtype)

    for lo in range(0, 2 * D_MODEL, TN_PROJ):
        o_ref[:, 3 * D_MODEL + lo:3 * D_MODEL + lo + TN_PROJ] = proj(6 * D_MODEL + lo).astype(o_ref.dtype)


def _inproj(x, norm_w, w_bf16, conv_w, layer, cos_t, sin_t, s):
    m = x.shape[0]
    assert TM_PROJ == TK and s % TM_PROJ == 0
    kern = functools.partial(_inproj_kernel, tiles_per_seq=s // TM_PROJ)
    return pl.pallas_call(
        kern,
        out_shape=(jax.ShapeDtypeStruct((m, PROJ_WIDTH), _BF16),
                   jax.ShapeDtypeStruct((m // TK, D_MODEL, TK), _BF16)),
        grid=(m // TM_PROJ,),
        in_specs=[
            pl.BlockSpec((TM_PROJ, D_MODEL), lambda i: (i, 0)),
            _layer_spec((1, D_MODEL), layer),
            _layer_spec((D_MODEL, IN_WIDTH), layer),
            pl.BlockSpec((TM_PROJ, LANES), lambda i: (i, 0)),
            pl.BlockSpec((TM_PROJ, LANES), lambda i: (i, 0)),
            _layer_spec((CONV_K, D_MODEL), layer),
        ],
        out_specs=(pl.BlockSpec((TM_PROJ, PROJ_WIDTH), lambda i: (i, 0)),
                   pl.BlockSpec((1, D_MODEL, TK), lambda i: (i, 0, 0))),
        scratch_shapes=[pltpu.VMEM((D_MODEL // TN_PROJ, TM_PROJ + SUBLANES, TN_PROJ), _F32),
                        pltpu.VMEM((SUBLANES, D_MODEL), _F32)],
        compiler_params=pltpu.CompilerParams(
            dimension_semantics=("arbitrary",), vmem_limit_bytes=VMEM_LIMIT),
        name="inproj",
    )(x, norm_w, w_bf16, cos_t, sin_t, conv_w)


def _attn_kernel(qf_ref, kf_ref, sf_ref, nf_ref, qp_ref, kp_ref, sp_ref, np_ref,
                 q_ref, k_ref, vt_ref, qcid_ref, kcid_ref, lam_ref, subw_ref, o_ref,
                 s_ref, mb_ref, m_ref, acc_ref, *, lambda_init, nq):
    b = pl.program_id(0)

    @pl.when((b == 0) & (pl.program_id(1) == 0))
    def _():
        m_ref[...] = jnp.full(m_ref.shape, NEG_INF, _F32)
        acc_ref[...] = jnp.zeros(acc_ref.shape, _F32)

    lane = lax.broadcasted_iota(jnp.int32, (TQ, V_DIM), 1)
    ones_rows = jnp.ones((VT_ROWS - V_DIM, TK), _BF16)

    def load_pair(qi, ki, masked):
        q = q_ref[pl.ds(pl.multiple_of(qi * TQ, TQ), TQ), :]
        zero = jnp.zeros_like(q)
        qms = (jnp.where(lane < HEAD_DIM, q, zero), jnp.where(lane >= HEAD_DIM, q, zero))
        ks = pl.multiple_of(ki * TK, TK)
        k = k_ref[pl.ds(ks, TK), :]
        allowed = None
        if masked:
            kc = kcid_ref[pl.ds(ks, TK), :]
            allowed = jnp.concatenate([kc] * (TQ // LANES), axis=1) <= qcid_ref[qi]
        return k, qms, allowed

    def scores(j, k, qms, allowed):
        s = lax.dot_general(k, qms[j], (((1,), (1,)), ((), ())), preferred_element_type=_F32)
        if allowed is not None:
            s = jnp.where(allowed, s, NEG_INF)
        s_ref[j] = s
        mb_ref[j] = jnp.max(s, axis=0, keepdims=True)

    def first_scores(ql_ref, kl_ref, masked):
        k0, qms0, allowed0 = load_pair(ql_ref[b, 0], kl_ref[b, 0], masked)
        for j in range(2):
            scores(j, k0, qms0, allowed0)

    def run_list(ql_ref, kl_ref, sl_ref, n, masked, unroll):
        @pl.when(n > 0)
        def _():
            def body(i, carry):
                for u in range(unroll):
                    t = i * unroll + u
                    slot = sl_ref[b, t]
                    vt = jnp.concatenate([vt_ref[kl_ref[b, t]], ones_rows], axis=0)
                    kn, qmsn, allowedn = load_pair(ql_ref[b, t + 1], kl_ref[b, t + 1],
                                                   masked or u == unroll - 1)
                    for j in range(2):
                        m_old = m_ref[j, slot]
                        m_new = jnp.maximum(m_old, mb_ref[j])
                        alpha = jnp.exp2(m_old - m_new)
                        p = jnp.exp2(s_ref[j] - m_new).astype(_BF16)
                        scores(j, kn, qmsn, allowedn)
                        acc_ref[j, slot] = (alpha * acc_ref[j, slot]
                                            + jnp.dot(vt, p, preferred_element_type=_F32))
                        m_ref[j, slot] = m_new
                return carry

            lax.fori_loop(0, (n + unroll - 1) // unroll, body, 0)

    n_full = nf_ref[b]

    @pl.when(n_full > 0)
    def _():
        first_scores(qf_ref, kf_ref, masked=False)

    @pl.when(n_full == 0)
    def _():
        first_scores(qp_ref, kp_ref, masked=True)

    run_list(qf_ref, kf_ref, sf_ref, n_full, masked=False, unroll=PAIR_UNROLL)
    run_list(qp_ref, kp_ref, sp_ref, np_ref[b], masked=True, unroll=PAIR_UNROLL_MASKED)

    lam_v = lam_ref[...]
    lam = (jnp.exp(jnp.sum(lam_v[0:1] * lam_v[1:2], axis=-1, keepdims=True))
           - jnp.exp(jnp.sum(lam_v[2:3] * lam_v[3:4], axis=-1, keepdims=True))
           + lambda_init)
    out_w = subw_ref[...] * (1.0 - lambda_init)

    def reset(slot):
        for j in range(2):
            m_ref[j, slot] = jnp.full((1, TQ), NEG_INF, _F32)
            acc_ref[j, slot] = jnp.zeros((VT_ROWS, TQ), _F32)

    def finish(qi, carry):
        a1 = acc_ref[0, qi]
        a2 = acc_ref[1, qi]
        o_t = (a1[0:V_DIM] * (1.0 / a1[V_DIM:V_DIM + 1])
               - a2[0:V_DIM] * (lam / a2[V_DIM:V_DIM + 1]))
        o_t = o_t * lax.rsqrt(jnp.mean(o_t * o_t, axis=0, keepdims=True) + NORM_EPS)
        o_ref[pl.ds(pl.multiple_of(qi * TQ, TQ), TQ), :] = (o_t.T * out_w).astype(o_ref.dtype)
        reset(qi)
        return carry

    lax.fori_loop(0, nq, finish, 0, unroll=4)
    reset(nq)


def _attn_schedule(positions):
    b, s = positions.shape
    nq, nk = s // TQ, s // TK
    cid = jnp.right_shift(positions, CHUNK_SHIFT)
    qc = cid.reshape(b, nq, TQ)
    kc = cid.reshape(b, nk, TK)
    qmin, qmax = qc.min(-1), qc.max(-1)
    kmin, kmax = kc.min(-1), kc.max(-1)
    need = kmin[:, None, :] <= qmax[:, :, None]
    full = kmax[:, None, :] <= qmin[:, :, None]
    part = need & ~full

    def pair_list(sel):
        sel = sel.reshape(b, nq * nk)
        n = sel.sum(-1).astype(jnp.int32)
        order = jnp.argsort(~sel, axis=-1, stable=True).astype(jnp.int32)
        order = jnp.pad(order, ((0, 0), (0, PAIR_UNROLL)))
        qb = order // nk
        slot = jnp.where(jnp.arange(order.shape[1])[None, :] < n[:, None], qb, nq)
        return qb, order % nk, slot, n

    qf, kf, sf, nf = pair_list(full)
    qp, kp, sp, npart = pair_list(part)
    rows = jnp.arange(b)
    after_last_trip = (nf + PAIR_UNROLL - 1) // PAIR_UNROLL * PAIR_UNROLL
    qf = qf.at[rows, after_last_trip].set(qp[:, 0])
    kf = kf.at[rows, after_last_trip].set(kp[:, 0])
    kcid_lanes = jnp.broadcast_to(cid.reshape(b * s, 1), (b * s, LANES))
    return (qf, kf, sf, nf, qp, kp, sp, npart, cid.reshape(b, nq, 1, TQ), kcid_lanes)


def _attn(proj, vt, sched, lam_params, subw, layer, lambda_init, b, s):
    qcid, kcid = sched[8:]
    nq, nk = s // TQ, s // TK
    kern = functools.partial(_attn_kernel, lambda_init=lambda_init, nq=nq)
    grid_spec = pltpu.PrefetchScalarGridSpec(
        num_scalar_prefetch=8,
        grid=(b, N_HEADS),
        in_specs=[
            pl.BlockSpec((s, V_DIM), lambda bi, h, *_: (bi, h)),
            pl.BlockSpec((s, V_DIM), lambda bi, h, *_: (bi, N_HEADS + h)),
            pl.BlockSpec((nk, V_DIM, TK), lambda bi, h, *_: (bi, h, 0)),
            pl.BlockSpec((None, nq, 1, TQ), lambda bi, h, *_: (bi, 0, 0, 0)),
            pl.BlockSpec((s, LANES), lambda bi, h, *_: (bi, 0)),
            pl.BlockSpec((None, 4, HEAD_DIM), lambda bi, h, *_: (layer, 0, 0)),
            pl.BlockSpec((None, 1, V_DIM), lambda bi, h, *_: (layer, 0, 0)),
        ],
        out_specs=pl.BlockSpec((s, V_DIM), lambda bi, h, *_: (bi, h)),
        scratch_shapes=[
            pltpu.VMEM((2, TK, TQ), _F32),
            pltpu.VMEM((2, 1, TQ), _F32),
            pltpu.VMEM((2, nq + 1, 1, TQ), _F32),
            pltpu.VMEM((2, nq + 1, VT_ROWS, TQ), _F32),
        ],
    )
    return pl.pallas_call(
        kern,
        out_shape=jax.ShapeDtypeStruct((b * s, N_HEADS * V_DIM), _BF16),
        grid_spec=grid_spec,
        compiler_params=pltpu.CompilerParams(
            dimension_semantics=("arbitrary", "arbitrary"),
            vmem_limit_bytes=VMEM_LIMIT),
        name="attn",
    )(*sched[:8], proj, proj, vt, qcid, kcid, lam_params, subw)


def _mix_kernel(x_ref, o_ref, yb_ref, ga_ref, gb_ref, wa_ref, wb_ref, wo_ref, out_ref):
    y_b = jnp.dot(yb_ref[...], wb_ref[...], preferred_element_type=_F32)
    y_a = jnp.dot(o_ref[...], wa_ref[...], preferred_element_type=_F32)
    mixed = (jax.nn.sigmoid(ga_ref[...].astype(_F32)) * y_a
             + jax.nn.sigmoid(gb_ref[...].astype(_F32)) * y_b)
    out_ref[...] = x_ref[...] + jnp.dot(mixed.astype(_BF16), wo_ref[...], preferred_element_type=_F32)


def _mix(x, attn_o, proj, wa, wb, wo, layer):
    m = x.shape[0]

    def col(c):
        return pl.BlockSpec((TM_MIX, D_MODEL), lambda i: (i, c))

    return pl.pallas_call(
        _mix_kernel,
        out_shape=jax.ShapeDtypeStruct((m, D_MODEL), _F32),
        grid=(m // TM_MIX,),
        in_specs=[col(0), col(0), col(2), col(3), col(4), _layer_spec((D_MODEL, D_MODEL), layer),
                  _layer_spec((D_MODEL, D_MODEL), layer), _layer_spec((D_MODEL, D_MODEL), layer)],
        out_specs=col(0),
        compiler_params=pltpu.CompilerParams(
            dimension_semantics=("arbitrary",), vmem_limit_bytes=VMEM_LIMIT),
        name="mix",
    )(x, attn_o, proj, proj, proj, wa, wb, wo)


def _ffn_kernel(x_ref, nw_ref, wg_ref, wu_ref, wd_ref, fw_ref, out_ref, *, final_norm):
    x = x_ref[...]
    hn = _rms(x, nw_ref[...]).astype(_BF16)
    acc = x
    for lo, hi in zip(FF_SPLITS[:-1], FF_SPLITS[1:]):
        sl = slice(lo, hi)
        g = jnp.dot(hn, wg_ref[:, sl], preferred_element_type=_F32)
        u = jnp.dot(hn, wu_ref[:, sl], preferred_element_type=_F32)
        h = (jax.nn.silu(g) * u).astype(_BF16)
        acc = acc + jnp.dot(h, wd_ref[sl, :], preferred_element_type=_F32)
    if final_norm:
        acc = _rms(acc, fw_ref[...])
    out_ref[...] = acc


def _ffn(x, norm_w, wg, wu, wd, final_w, layer, final_norm):
    m = x.shape[0]
    tile = pl.BlockSpec((TM_FFN, D_MODEL), lambda i: (i, 0))
    kern = functools.partial(_ffn_kernel, final_norm=final_norm)
    return pl.pallas_call(
        kern,
        out_shape=jax.ShapeDtypeStruct((m, D_MODEL), _F32),
        grid=(m // TM_FFN,),
        in_specs=[tile, _layer_spec((1, D_MODEL), layer), _layer_spec((D_MODEL, D_FF), layer),
                  _layer_spec((D_MODEL, D_FF), layer), _layer_spec((D_FF, D_MODEL), layer),
                  pl.BlockSpec((1, D_MODEL), lambda i: (0, 0))],
        out_specs=tile,
        compiler_params=pltpu.CompilerParams(
            dimension_semantics=("arbitrary",), vmem_limit_bytes=VMEM_LIMIT),
        name="ffn",
    )(x, norm_w, wg, wu, wd, final_w.reshape(1, D_MODEL))


def kernel(x, positions, mix_norm, w_in, lambda_q1, lambda_k1, lambda_q2, lambda_k2, subln_w, conv_w,
           w_branch_a, w_branch_b, w_out, ffn_norm, w_gate, w_up, w_down, final_norm):
    b, s, d = x.shape
    assert d == D_MODEL and s % TQ == 0 and s % TM_MIX == 0 and (b * s) % TM_PROJ == 0
    h = x.reshape(b * s, d)
    cos_t, sin_t = _rope_tables(positions)
    sched = _attn_schedule(positions)
    mix_norm3 = mix_norm.reshape(DEPTH, 1, D_MODEL)
    ffn_norm3 = ffn_norm.reshape(DEPTH, 1, D_MODEL)
    subln3 = subln_w.reshape(DEPTH, 1, V_DIM)
    lam_params = jnp.stack([lambda_q1, lambda_k1, lambda_q2, lambda_k2], axis=1).astype(_F32)
    w_in, w_branch_a, w_branch_b, w_out, w_gate, w_up, w_down = (
        w.astype(_BF16) for w in (w_in, w_branch_a, w_branch_b, w_out, w_gate, w_up, w_down))
    for l in range(DEPTH):
        lambda_init = 0.8 - 0.6 * math.exp(-0.3 * l)
        proj, vt = _inproj(h, mix_norm3, w_in, conv_w, l, cos_t, sin_t, s)
        attn_o = _attn(proj, vt, sched, lam_params, subln3, l, lambda_init, b, s)
        h = _mix(h, attn_o, proj, w_branch_a, w_branch_b, w_out, l)
        h = _ffn(h, ffn_norm3, w_gate, w_up, w_down, final_norm, l, l == DEPTH - 1)
    return h.reshape(b, s, d)
```

```python
import functools
import math

import jax
import jax.numpy as jnp
from jax import lax
from jax.experimental import pallas as pl
from jax.experimental.pallas import tpu as pltpu

D_MODEL = 1024
DEPTH = 4
CHUNK_SHIFT = 6
N_HEADS = 8
HEAD_DIM = 64
V_DIM = 2 * HEAD_DIM
ROT_DIM = HEAD_DIM // 4
ROPE_THETA = 500000.0
CONV_K = 3
D_FF = 2816
IN_WIDTH = 8 * D_MODEL
PROJ_WIDTH = 5 * D_MODEL
NORM_EPS = 1e-6
NEG_INF = -1e30
LOG2_E = 1.4426950408889634

LANES = 128
SUBLANES = 8
BF16_ROWS = 16
VMEM_LIMIT = 56 * 1024 * 1024

TM_PROJ = 512
TN_PROJ = 512
TQ = 512
TK = 512
VT_ROWS = V_DIM + BF16_ROWS
PAIR_UNROLL = 14
PAIR_UNROLL_MASKED = 8
TM_MIX = 512
TM_FFN = 512
MXU_DIM = 256
FF_SPLITS = (0, 6 * MXU_DIM, D_FF)

_F32 = jnp.float32
_BF16 = jnp.bfloat16


def _rms(x, w):
    return x * lax.rsqrt(jnp.mean(x * x, axis=-1, keepdims=True) + NORM_EPS) * w


def _layer_spec(shape, layer):
    return pl.BlockSpec((None,) + shape, lambda *_: (layer,) + (0,) * len(shape),
                        pipeline_mode=pl.Buffered(1))


def _rope_table_kernel(ang_ref, cos_ref, sin_ref, nsin_ref):
    ang = ang_ref[...]
    s = jnp.sin(ang)
    cos_ref[...] = jnp.cos(ang)
    sin_ref[...] = s
    nsin_ref[...] = -s


def _rope_tables(positions):
    m = positions.size
    nf = ROT_DIM // 2
    per_row = LANES // nf
    freqs = ROPE_THETA ** (-jnp.arange(0, ROT_DIM, 2, dtype=_F32) / ROT_DIM)
    ang = (positions.reshape(m, 1).astype(_F32) * freqs[None, :]).reshape(m // per_row, LANES)
    cos, sin, nsin = pl.pallas_call(
        _rope_table_kernel,
        out_shape=(jax.ShapeDtypeStruct(ang.shape, _F32),) * 3,
        name="rope_tables",
    )(ang)
    cos, sin, nsin = (t.reshape(m, nf) for t in (cos, sin, nsin))
    rest = HEAD_DIM - ROT_DIM
    cos_map = jnp.concatenate([cos, cos, jnp.ones((m, rest), _F32)], axis=1)
    sin_map = jnp.concatenate([nsin, sin, jnp.zeros((m, rest), _F32)], axis=1)
    return jnp.tile(cos_map, (1, LANES // HEAD_DIM)), jnp.tile(sin_map, (1, LANES // HEAD_DIM))


def _inproj_kernel(x_ref, nw_ref, w_ref, cos_ref, sin_ref, cw_ref, o_ref, vt_ref, ext_ref, carry_ref,
                   *, tiles_per_seq):
    i = pl.program_id(0)

    @pl.when(i == 0)
    def _():
        carry_ref[...] = jnp.zeros(carry_ref.shape, _F32)

    xn = _rms(x_ref[...], nw_ref[...]).astype(_BF16)

    def proj(lo):
        return jnp.dot(xn, w_ref[:, lo:lo + TN_PROJ], preferred_element_type=_F32)

    cos = cos_ref[...]
    sin = sin_ref[...]
    first_half = (lax.broadcasted_iota(jnp.int32, cos.shape, 1) % HEAD_DIM) < ROT_DIM // 2
    q_scale = HEAD_DIM ** -0.5 * LOG2_E
    for lo in range(0, 2 * D_MODEL, TN_PROJ):
        acc = proj(lo)
        cs, sn = (cos * q_scale, sin * q_scale) if lo < D_MODEL else (cos, sin)
        for c in range(TN_PROJ // LANES):
            t = acc[:, c * LANES:(c + 1) * LANES]
            partner = jnp.where(first_half,
                                pltpu.roll(t, LANES - ROT_DIM // 2, axis=1),
                                pltpu.roll(t, ROT_DIM // 2, axis=1))
            o_ref[:, lo + c * LANES:lo + (c + 1) * LANES] = (t * cs + partner * sn).astype(o_ref.dtype)

    for lo in range(0, D_MODEL, TN_PROJ):
        vt_ref[0, lo:lo + TN_PROJ, :] = proj(2 * D_MODEL + lo).T.astype(vt_ref.dtype)

    seq_start = i % tiles_per_seq == 0
    for n, lo in enumerate(range(0, D_MODEL, TN_PROJ)):
        cols = slice(lo, lo + TN_PROJ)
        u = proj(5 * D_MODEL + lo)
        cu = proj(4 * D_MODEL + lo) * u
        ext_ref[n, 0:SUBLANES, :] = jnp.where(seq_start, 0.0, carry_ref[:, cols])
        ext_ref[n, SUBLANES:, :] = cu
        carry_ref[:, cols] = cu[TM_PROJ - SUBLANES:, :]
        cw = cw_ref[:, cols]
        conv = (cw[0:1] * ext_ref[n, SUBLANES - 2:SUBLANES - 2 + TM_PROJ, :]
                + cw[1:2] * ext_ref[n, SUBLANES - 1:SUBLANES - 1 + TM_PROJ, :]
                + cw[2:3] * ext_ref[n, SUBLANES:, :])
        b_gate = proj(3 * D_MODEL + lo)
        o_ref[:, 2 * D_MODEL + lo:2 * D_MODEL + lo + TN_PROJ] = (b_gate * conv).astype(o_ref.dtype)

    for lo in range(0, 2 * D_MODEL, TN_PROJ):
        o_ref[:, 3 * D_MODEL + lo:3 * D_MODEL + lo + TN_PROJ] = proj(6 * D_MODEL + lo).astype(o_ref.dtype)


def _inproj(x, norm_w, w_bf16, conv_w, layer, cos_t, sin_t, s):
    m = x.shape[0]
    assert TM_PROJ == TK and s % TM_PROJ == 0
    kern = functools.partial(_inproj_kernel, tiles_per_seq=s // TM_PROJ)
    return pl.pallas_call(
        kern,
        out_shape=(jax.ShapeDtypeStruct((m, PROJ_WIDTH), _BF16),
                   jax.ShapeDtypeStruct((m // TK, D_MODEL, TK), _BF16)),
        grid=(m // TM_PROJ,),
        in_specs=[
            pl.BlockSpec((TM_PROJ, D_MODEL), lambda i: (i, 0)),
            _layer_spec((1, D_MODEL), layer),
            _layer_spec((D_MODEL, IN_WIDTH), layer),
            pl.BlockSpec((TM_PROJ, LANES), lambda i: (i, 0)),
            pl.BlockSpec((TM_PROJ, LANES), lambda i: (i, 0)),
            _layer_spec((CONV_K, D_MODEL), layer),
        ],
        out_specs=(pl.BlockSpec((TM_PROJ, PROJ_WIDTH), lambda i: (i, 0)),
                   pl.BlockSpec((1, D_MODEL, TK), lambda i: (i, 0, 0))),
        scratch_shapes=[pltpu.VMEM((D_MODEL // TN_PROJ, TM_PROJ + SUBLANES, TN_PROJ), _F32),
                        pltpu.VMEM((SUBLANES, D_MODEL), _F32)],
        compiler_params=pltpu.CompilerParams(
            dimension_semantics=("arbitrary",), vmem_limit_bytes=VMEM_LIMIT),
        name="inproj",
    )(x, norm_w, w_bf16, cos_t, sin_t, conv_w)


def _attn_kernel(qf_ref, kf_ref, sf_ref, nf_ref, qp_ref, kp_ref, sp_ref, np_ref,
                 q_ref, k_ref, vt_ref, qcid_ref, kcid_ref, lam_ref, subw_ref, o_ref,
                 s_ref, mb_ref, m_ref, acc_ref, *, lambda_init, nq):
    b = pl.program_id(0)

    @pl.when((b == 0) & (pl.program_id(1) == 0))
    def _():
        m_ref[...] = jnp.full(m_ref.shape, NEG_INF, _F32)
        acc_ref[...] = jnp.zeros(acc_ref.shape, _F32)

    lane = lax.broadcasted_iota(jnp.int32, (TQ, V_DIM), 1)
    ones_rows = jnp.ones((VT_ROWS - V_DIM, TK), _BF16)

    def load_pair(qi, ki, masked):
        q = q_ref[pl.ds(pl.multiple_of(qi * TQ, TQ), TQ), :]
        zero = jnp.zeros_like(q)
        qms = (jnp.where(lane < HEAD_DIM, q, zero), jnp.where(lane >= HEAD_DIM, q, zero))
        ks = pl.multiple_of(ki * TK, TK)
        k = k_ref[pl.ds(ks, TK), :]
        cids = None
        if masked:
            kc = kcid_ref[pl.ds(ks, TK), :]
            cids = (jnp.concatenate([kc] * (TQ // LANES), axis=1), qcid_ref[qi])
        return k, qms, cids

    def scores(j, k, qms, cids):
        s = lax.dot_general(k, qms[j], (((1,), (1,)), ((), ())), preferred_element_type=_F32)
        if cids is not None:
            kc, qc = cids
            s = jnp.where(kc <= qc, s, NEG_INF) if j == 0 else jnp.where(kc > qc, NEG_INF, s)
        s_ref[j] = s
        mb_ref[j] = jnp.max(s, axis=0, keepdims=True)

    def first_scores(ql_ref, kl_ref, masked):
        k0, qms0, allowed0 = load_pair(ql_ref[b, 0], kl_ref[b, 0], masked)
        for j in range(2):
            scores(j, k0, qms0, allowed0)

    def run_list(ql_ref, kl_ref, sl_ref, n, masked, unroll):
        @pl.when(n > 0)
        def _():
            def body(i, carry):
                for u in range(unroll):
                    t = i * unroll + u
                    slot = sl_ref[b, t]
                    vt = jnp.concatenate([vt_ref[kl_ref[b, t]], ones_rows], axis=0)
                    kn, qmsn, allowedn = load_pair(ql_ref[b, t + 1], kl_ref[b, t + 1],
                                                   masked or u == unroll - 1)
                    for j in range(2):
                        m_old = m_ref[j, slot]
                        m_new = jnp.maximum(m_old, mb_ref[j])
                        alpha = jnp.exp2(m_old - m_new)
                        p = jnp.exp2(s_ref[j] - m_new).astype(_BF16)
                        scores(j, kn, qmsn, allowedn)
                        acc_ref[j, slot] = (alpha * acc_ref[j, slot]
                                            + jnp.dot(vt, p, preferred_element_type=_F32))
                        m_ref[j, slot] = m_new
                return carry

            lax.fori_loop(0, (n + unroll - 1) // unroll, body, 0)

    n_full = nf_ref[b]

    @pl.when(n_full > 0)
    def _():
        first_scores(qf_ref, kf_ref, masked=False)

    @pl.when(n_full == 0)
    def _():
        first_scores(qp_ref, kp_ref, masked=True)

    run_list(qf_ref, kf_ref, sf_ref, n_full, masked=False, unroll=PAIR_UNROLL)
    run_list(qp_ref, kp_ref, sp_ref, np_ref[b], masked=True, unroll=PAIR_UNROLL_MASKED)

    lam_v = lam_ref[...]
    lam = (jnp.exp(jnp.sum(lam_v[0:1] * lam_v[1:2], axis=-1, keepdims=True))
           - jnp.exp(jnp.sum(lam_v[2:3] * lam_v[3:4], axis=-1, keepdims=True))
           + lambda_init)
    out_w = subw_ref[...] * (1.0 - lambda_init)

    def reset(slot):
        for j in range(2):
            m_ref[j, slot] = jnp.full((1, TQ), NEG_INF, _F32)
            acc_ref[j, slot] = jnp.zeros((VT_ROWS, TQ), _F32)

    def finish(qi, carry):
        a1 = acc_ref[0, qi]
        a2 = acc_ref[1, qi]
        o_t = (a1[0:V_DIM] * (1.0 / a1[V_DIM:V_DIM + 1])
               - a2[0:V_DIM] * (lam / a2[V_DIM:V_DIM + 1]))
        o_t = o_t * lax.rsqrt(jnp.mean(o_t * o_t, axis=0, keepdims=True) + NORM_EPS)
        o_ref[pl.ds(pl.multiple_of(qi * TQ, TQ), TQ), :] = (o_t.T * out_w).astype(o_ref.dtype)
        reset(qi)
        return carry

    lax.fori_loop(0, nq, finish, 0, unroll=4)
    reset(nq)


def _attn_schedule(positions):
    b, s = positions.shape
    nq, nk = s // TQ, s // TK
    cid = jnp.right_shift(positions, CHUNK_SHIFT)
    qc = cid.reshape(b, nq, TQ)
    kc = cid.reshape(b, nk, TK)
    qmin, qmax = qc.min(-1), qc.max(-1)
    kmin, kmax = kc.min(-1), kc.max(-1)
    need = kmin[:, None, :] <= qmax[:, :, None]
    full = kmax[:, None, :] <= qmin[:, :, None]
    part = need & ~full

    def pair_list(sel):
        sel = sel.reshape(b, nq * nk)
        n = sel.sum(-1).astype(jnp.int32)
        order = jnp.argsort(~sel, axis=-1, stable=True).astype(jnp.int32)
        order = jnp.pad(order, ((0, 0), (0, PAIR_UNROLL)))
        qb = order // nk
        slot = jnp.where(jnp.arange(order.shape[1])[None, :] < n[:, None], qb, nq)
        return qb, order % nk, slot, n

    qf, kf, sf, nf = pair_list(full)
    qp, kp, sp, npart = pair_list(part)
    rows = jnp.arange(b)
    after_last_trip = (nf + PAIR_UNROLL - 1) // PAIR_UNROLL * PAIR_UNROLL
    qf = qf.at[rows, after_last_trip].set(qp[:, 0])
    kf = kf.at[rows, after_last_trip].set(kp[:, 0])
    kcid_lanes = jnp.broadcast_to(cid.reshape(b * s, 1), (b * s, LANES))
    return (qf, kf, sf, nf, qp, kp, sp, npart, cid.reshape(b, nq, 1, TQ), kcid_lanes)


def _attn(proj, vt, sched, lam_params, subw, layer, lambda_init, b, s):
    qcid, kcid = sched[8:]
    nq, nk = s // TQ, s // TK
    kern = functools.partial(_attn_kernel, lambda_init=lambda_init, nq=nq)
    grid_spec = pltpu.PrefetchScalarGridSpec(
        num_scalar_prefetch=8,
        grid=(b, N_HEADS),
        in_specs=[
            pl.BlockSpec((s, V_DIM), lambda bi, h, *_: (bi, h)),
            pl.BlockSpec((s, V_DIM), lambda bi, h, *_: (bi, N_HEADS + h)),
            pl.BlockSpec((nk, V_DIM, TK), lambda bi, h, *_: (bi, h, 0)),
            pl.BlockSpec((None, nq, 1, TQ), lambda bi, h, *_: (bi, 0, 0, 0)),
            pl.BlockSpec((s, LANES), lambda bi, h, *_: (bi, 0)),
            pl.BlockSpec((None, 4, HEAD_DIM), lambda bi, h, *_: (layer, 0, 0)),
            pl.BlockSpec((None, 1, V_DIM), lambda bi, h, *_: (layer, 0, 0)),
        ],
        out_specs=pl.BlockSpec((s, V_DIM), lambda bi, h, *_: (bi, h)),
        scratch_shapes=[
            pltpu.VMEM((2, TK, TQ), _F32),
            pltpu.VMEM((2, 1, TQ), _F32),
            pltpu.VMEM((2, nq + 1, 1, TQ), _F32),
            pltpu.VMEM((2, nq + 1, VT_ROWS, TQ), _F32),
        ],
    )
    return pl.pallas_call(
        kern,
        out_shape=jax.ShapeDtypeStruct((b * s, N_HEADS * V_DIM), _BF16),
        grid_spec=grid_spec,
        compiler_params=pltpu.CompilerParams(
            dimension_semantics=("arbitrary", "arbitrary"),
            vmem_limit_bytes=VMEM_LIMIT),
        name="attn",
    )(*sched[:8], proj, proj, vt, qcid, kcid, lam_params, subw)


def _mix_kernel(x_ref, o_ref, yb_ref, ga_ref, gb_ref, wa_ref, wb_ref, wo_ref, out_ref):
    y_b = jnp.dot(yb_ref[...], wb_ref[...], preferred_element_type=_F32)
    y_a = jnp.dot(o_ref[...], wa_ref[...], preferred_element_type=_F32)
    mixed = (jax.nn.sigmoid(ga_ref[...].astype(_F32)) * y_a
             + jax.nn.sigmoid(gb_ref[...].astype(_F32)) * y_b)
    out_ref[...] = x_ref[...] + jnp.dot(mixed.astype(_BF16), wo_ref[...], preferred_element_type=_F32)


def _mix(x, attn_o, proj, wa, wb, wo, layer):
    m = x.shape[0]

    def col(c):
        return pl.BlockSpec((TM_MIX, D_MODEL), lambda i: (i, c))

    return pl.pallas_call(
        _mix_kernel,
        out_shape=jax.ShapeDtypeStruct((m, D_MODEL), _F32),
        grid=(m // TM_MIX,),
        in_specs=[col(0), col(0), col(2), col(3), col(4), _layer_spec((D_MODEL, D_MODEL), layer),
                  _layer_spec((D_MODEL, D_MODEL), layer), _layer_spec((D_MODEL, D_MODEL), layer)],
        out_specs=col(0),
        compiler_params=pltpu.CompilerParams(
            dimension_semantics=("arbitrary",), vmem_limit_bytes=VMEM_LIMIT),
        name="mix",
    )(x, attn_o, proj, proj, proj, wa, wb, wo)


def _ffn_kernel(x_ref, nw_ref, wg_ref, wu_ref, wd_ref, fw_ref, out_ref, *, final_norm):
    x = x_ref[...]
    hn = _rms(x, nw_ref[...]).astype(_BF16)
    acc = x
    for lo, hi in zip(FF_SPLITS[:-1], FF_SPLITS[1:]):
        sl = slice(lo, hi)
        g = jnp.dot(hn, wg_ref[:, sl], preferred_element_type=_F32)
        u = jnp.dot(hn, wu_ref[:, sl], preferred_element_type=_F32)
        h = (jax.nn.silu(g) * u).astype(_BF16)
        acc = acc + jnp.dot(h, wd_ref[sl, :], preferred_element_type=_F32)
    if final_norm:
        acc = _rms(acc, fw_ref[...])
    out_ref[...] = acc


def _ffn(x, norm_w, wg, wu, wd, final_w, layer, final_norm):
    m = x.shape[0]
    tile = pl.BlockSpec((TM_FFN, D_MODEL), lambda i: (i, 0))
    kern = functools.partial(_ffn_kernel, final_norm=final_norm)
    return pl.pallas_call(
        kern,
        out_shape=jax.ShapeDtypeStruct((m, D_MODEL), _F32),
        grid=(m // TM_FFN,),
        in_specs=[tile, _layer_spec((1, D_MODEL), layer), _layer_spec((D_MODEL, D_FF), layer),
                  _layer_spec((D_MODEL, D_FF), layer), _layer_spec((D_FF, D_MODEL), layer),
                  pl.BlockSpec((1, D_MODEL), lambda i: (0, 0))],
        out_specs=tile,
        compiler_params=pltpu.CompilerParams(
            dimension_semantics=("arbitrary",), vmem_limit_bytes=VMEM_LIMIT),
        name="ffn",
    )(x, norm_w, wg, wu, wd, final_w.reshape(1, D_MODEL))


def kernel(x, positions, mix_norm, w_in, lambda_q1, lambda_k1, lambda_q2, lambda_k2, subln_w, conv_w,
           w_branch_a, w_branch_b, w_out, ffn_norm, w_gate, w_up, w_down, final_norm):
    b, s, d = x.shape
    assert d == D_MODEL and s % TQ == 0 and s % TM_MIX == 0 and (b * s) % TM_PROJ == 0
    h = x.reshape(b * s, d)
    cos_t, sin_t = _rope_tables(positions)
    sched = _attn_schedule(positions)
    mix_norm3 = mix_norm.reshape(DEPTH, 1, D_MODEL)
    ffn_norm3 = ffn_norm.reshape(DEPTH, 1, D_MODEL)
    subln3 = subln_w.reshape(DEPTH, 1, V_DIM)
    lam_params = jnp.stack([lambda_q1, lambda_k1, lambda_q2, lambda_k2], axis=1).astype(_F32)
    w_in, w_branch_a, w_branch_b, w_out, w_gate, w_up, w_down = (
        w.astype(_BF16) for w in (w_in, w_branch_a, w_branch_b, w_out, w_gate, w_up, w_down))
    for l in range(DEPTH):
        lambda_init = 0.8 - 0.6 * math.exp(-0.3 * l)
        proj, vt = _inproj(h, mix_norm3, w_in, conv_w, l, cos_t, sin_t, s)
        attn_o = _attn(proj, vt, sched, lam_params, subln3, l, lambda_init, b, s)
        h = _mix(h, attn_o, proj, w_branch_a, w_branch_b, w_out, l)
        h = _ffn(h, ffn_norm3, w_gate, w_up, w_down, final_norm, l, l == DEPTH - 1)
    return h.reshape(b, s, d)
```

```python
import functools
import math

import jax
import jax.numpy as jnp
from jax import lax
from jax.experimental import pallas as pl
from jax.experimental.pallas import tpu as pltpu

D_MODEL = 1024
DEPTH = 4
CHUNK_SHIFT = 6
N_HEADS = 8
HEAD_DIM = 64
V_DIM = 2 * HEAD_DIM
ROT_DIM = HEAD_DIM // 4
ROPE_THETA = 500000.0
CONV_K = 3
D_FF = 2816
IN_WIDTH = 8 * D_MODEL
PROJ_WIDTH = 5 * D_MODEL
NORM_EPS = 1e-6
NEG_INF = -1e30
LOG2_E = 1.4426950408889634

LANES = 128
SUBLANES = 8
BF16_ROWS = 16
VMEM_LIMIT = 56 * 1024 * 1024

TM_PROJ = 512
TN_PROJ = 512
TQ = 512
TK = 512
VT_ROWS = V_DIM + BF16_ROWS
PAIR_UNROLL = 14
PAIR_UNROLL_MASKED = 8
TM_MIX = 512
TM_FFN = 512
MXU_DIM = 256
FF_SPLITS = (0, 6 * MXU_DIM, D_FF)

_F32 = jnp.float32
_BF16 = jnp.bfloat16


def _rms(x, w):
    return x * lax.rsqrt(jnp.mean(x * x, axis=-1, keepdims=True) + NORM_EPS) * w


def _layer_spec(shape, layer):
    return pl.BlockSpec((None,) + shape, lambda *_: (layer,) + (0,) * len(shape),
                        pipeline_mode=pl.Buffered(1))


def _rope_table_kernel(ang_ref, cos_ref, sin_ref):
    ang = ang_ref[...]
    p = lax.broadcasted_iota(jnp.int32, ang.shape, 1) % HEAD_DIM
    c = jnp.cos(ang)
    s = jnp.sin(ang)
    cos_ref[...] = jnp.where(p < ROT_DIM, c, 1.0)
    sin_ref[...] = jnp.where(p < ROT_DIM // 2, -s, jnp.where(p < ROT_DIM, s, 0.0))


def _rope_tables(positions):
    m = positions.size
    freqs = ROPE_THETA ** (-jnp.arange(0, ROT_DIM, 2, dtype=_F32) / ROT_DIM)
    lane_freq = jnp.tile(freqs, LANES // (ROT_DIM // 2))
    ang = positions.reshape(m, 1).astype(_F32) * lane_freq[None, :]
    tm = 2048
    spec = pl.BlockSpec((tm, LANES), lambda i: (i, 0))
    return pl.pallas_call(
        _rope_table_kernel,
        out_shape=(jax.ShapeDtypeStruct((m, LANES), _F32),) * 2,
        grid=(m // tm,),
        in_specs=[spec],
        out_specs=(spec, spec),
        name="rope_tables",
    )(ang)


def _inproj_kernel(x_ref, nw_ref, w_ref, cos_ref, sin_ref, cw_ref, o_ref, vt_ref, ext_ref, carry_ref,
                   *, tiles_per_seq):
    i = pl.program_id(0)

    @pl.when(i == 0)
    def _():
        carry_ref[...] = jnp.zeros(carry_ref.shape, _F32)

    xn = _rms(x_ref[...], nw_ref[...]).astype(_BF16)

    def proj(lo):
        return jnp.dot(xn, w_ref[:, lo:lo + TN_PROJ], preferred_element_type=_F32)

    cos = cos_ref[...]
    sin = sin_ref[...]
    first_half = (lax.broadcasted_iota(jnp.int32, cos.shape, 1) % HEAD_DIM) < ROT_DIM // 2
    q_scale = HEAD_DIM ** -0.5 * LOG2_E
    for lo in range(0, 2 * D_MODEL, TN_PROJ):
        acc = proj(lo)
        cs, sn = (cos * q_scale, sin * q_scale) if lo < D_MODEL else (cos, sin)
        for c in range(TN_PROJ // LANES):
            t = acc[:, c * LANES:(c + 1) * LANES]
            partner = jnp.where(first_half,
                                pltpu.roll(t, LANES - ROT_DIM // 2, axis=1),
                                pltpu.roll(t, ROT_DIM // 2, axis=1))
            o_ref[:, lo + c * LANES:lo + (c + 1) * LANES] = (t * cs + partner * sn).astype(o_ref.dtype)

    for lo in range(0, D_MODEL, TN_PROJ):
        vt_ref[0, lo:lo + TN_PROJ, :] = proj(2 * D_MODEL + lo).T.astype(vt_ref.dtype)

    seq_start = i % tiles_per_seq == 0
    for n, lo in enumerate(range(0, D_MODEL, TN_PROJ)):
        cols = slice(lo, lo + TN_PROJ)
        u = proj(5 * D_MODEL + lo)
        cu = proj(4 * D_MODEL + lo) * u
        ext_ref[n, 0:SUBLANES, :] = jnp.where(seq_start, 0.0, carry_ref[:, cols])
        ext_ref[n, SUBLANES:, :] = cu
        carry_ref[:, cols] = cu[TM_PROJ - SUBLANES:, :]
        cw = cw_ref[:, cols]
        conv = (cw[0:1] * ext_ref[n, SUBLANES - 2:SUBLANES - 2 + TM_PROJ, :]
                + cw[1:2] * ext_ref[n, SUBLANES - 1:SUBLANES - 1 + TM_PROJ, :]
                + cw[2:3] * ext_ref[n, SUBLANES:, :])
        b_gate = proj(3 * D_MODEL + lo)
        o_ref[:, 2 * D_MODEL + lo:2 * D_MODEL + lo + TN_PROJ] = (b_gate * conv).astype(o_ref.dtype)

    for lo in range(0, 2 * D_MODEL, TN_PROJ):
        o_ref[:, 3 * D_MODEL + lo:3 * D_MODEL + lo + TN_PROJ] = proj(6 * D_MODEL + lo).astype(o_ref.dtype)


def _inproj(x, norm_w, w_bf16, conv_w, layer, cos_t, sin_t, s):
    m = x.shape[0]
    assert TM_PROJ == TK and s % TM_PROJ == 0
    kern = functools.partial(_inproj_kernel, tiles_per_seq=s // TM_PROJ)
    return pl.pallas_call(
        kern,
        out_shape=(jax.ShapeDtypeStruct((m, PROJ_WIDTH), _BF16),
                   jax.ShapeDtypeStruct((m // TK, D_MODEL, TK), _BF16)),
        grid=(m // TM_PROJ,),
        in_specs=[
            pl.BlockSpec((TM_PROJ, D_MODEL), lambda i: (i, 0)),
            _layer_spec((1, D_MODEL), layer),
            _layer_spec((D_MODEL, IN_WIDTH), layer),
            pl.BlockSpec((TM_PROJ, LANES), lambda i: (i, 0)),
            pl.BlockSpec((TM_PROJ, LANES), lambda i: (i, 0)),
            _layer_spec((CONV_K, D_MODEL), layer),
        ],
        out_specs=(pl.BlockSpec((TM_PROJ, PROJ_WIDTH), lambda i: (i, 0)),
                   pl.BlockSpec((1, D_MODEL, TK), lambda i: (i, 0, 0))),
        scratch_shapes=[pltpu.VMEM((D_MODEL // TN_PROJ, TM_PROJ + SUBLANES, TN_PROJ), _F32),
                        pltpu.VMEM((SUBLANES, D_MODEL), _F32)],
        compiler_params=pltpu.CompilerParams(
            dimension_semantics=("arbitrary",), vmem_limit_bytes=VMEM_LIMIT),
        name="inproj",
    )(x, norm_w, w_bf16, cos_t, sin_t, conv_w)


def _attn_kernel(qf_ref, kf_ref, sf_ref, nf_ref, qp_ref, kp_ref, sp_ref, np_ref,
                 q_ref, k_ref, vt_ref, qcid_ref, kcid_ref, lam_ref, subw_ref, o_ref,
                 s_ref, mb_ref, m_ref, acc_ref, *, lambda_init, nq):
    b = pl.program_id(0)

    @pl.when((b == 0) & (pl.program_id(1) == 0))
    def _():
        m_ref[...] = jnp.full(m_ref.shape, NEG_INF, _F32)
        acc_ref[...] = jnp.zeros(acc_ref.shape, _F32)

    lane = lax.broadcasted_iota(jnp.int32, (TQ, V_DIM), 1)
    ones_rows = jnp.ones((VT_ROWS - V_DIM, TK), _BF16)

    def load_pair(qi, ki, masked):
        q = q_ref[pl.ds(pl.multiple_of(qi * TQ, TQ), TQ), :]
        zero = jnp.zeros_like(q)
        qms = (jnp.where(lane < HEAD_DIM, q, zero), jnp.where(lane >= HEAD_DIM, q, zero))
        ks = pl.multiple_of(ki * TK, TK)
        k = k_ref[pl.ds(ks, TK), :]
        cids = None
        if masked:
            kc = kcid_ref[pl.ds(ks, TK), :]
            cids = (jnp.concatenate([kc] * (TQ // LANES), axis=1), qcid_ref[qi])
        return k, qms, cids

    def scores(j, k, qms, cids):
        s = lax.dot_general(k, qms[j], (((1,), (1,)), ((), ())), preferred_element_type=_F32)
        if cids is not None:
            kc, qc = cids
            s = jnp.where(kc <= qc, s, NEG_INF) if j == 0 else jnp.where(kc > qc, NEG_INF, s)
        s_ref[j] = s
        mb_ref[j] = jnp.max(s, axis=0, keepdims=True)

    def first_scores(ql_ref, kl_ref, masked):
        k0, qms0, allowed0 = load_pair(ql_ref[b, 0], kl_ref[b, 0], masked)
        for j in range(2):
            scores(j, k0, qms0, allowed0)

    def run_list(ql_ref, kl_ref, sl_ref, n, masked, unroll):
        @pl.when(n > 0)
        def _():
            def body(i, carry):
                for u in range(unroll):
                    t = i * unroll + u
                    slot = sl_ref[b, t]
                    vt = jnp.concatenate([vt_ref[kl_ref[b, t]], ones_rows], axis=0)
                    kn, qmsn, allowedn = load_pair(ql_ref[b, t + 1], kl_ref[b, t + 1],
                                                   masked or u == unroll - 1)
                    for j in range(2):
                        m_old = m_ref[j, slot]
                        m_new = jnp.maximum(m_old, mb_ref[j])
                        alpha = jnp.exp2(m_old - m_new)
                        p = jnp.exp2(s_ref[j] - m_new).astype(_BF16)
                        scores(j, kn, qmsn, allowedn)
                        acc_ref[j, slot] = (alpha * acc_ref[j, slot]
                                            + jnp.dot(vt, p, preferred_element_type=_F32))
                        m_ref[j, slot] = m_new
                return carry

            lax.fori_loop(0, (n + unroll - 1) // unroll, body, 0)

    n_full = nf_ref[b]

    @pl.when(n_full > 0)
    def _():
        first_scores(qf_ref, kf_ref, masked=False)

    @pl.when(n_full == 0)
    def _():
        first_scores(qp_ref, kp_ref, masked=True)

    run_list(qf_ref, kf_ref, sf_ref, n_full, masked=False, unroll=PAIR_UNROLL)
    run_list(qp_ref, kp_ref, sp_ref, np_ref[b], masked=True, unroll=PAIR_UNROLL_MASKED)

    lam_v = lam_ref[...]
    lam = (jnp.exp(jnp.sum(lam_v[0:1] * lam_v[1:2], axis=-1, keepdims=True))
           - jnp.exp(jnp.sum(lam_v[2:3] * lam_v[3:4], axis=-1, keepdims=True))
           + lambda_init)
    out_w = subw_ref[...] * (1.0 - lambda_init)

    def reset(slot):
        for j in range(2):
            m_ref[j, slot] = jnp.full((1, TQ), NEG_INF, _F32)
            acc_ref[j, slot] = jnp.zeros((VT_ROWS, TQ), _F32)

    def finish(qi, carry):
        a1 = acc_ref[0, qi]
        a2 = acc_ref[1, qi]
        o_t = (a1[0:V_DIM] * (1.0 / a1[V_DIM:V_DIM + 1])
               - a2[0:V_DIM] * (lam / a2[V_DIM:V_DIM + 1]))
        o_t = o_t * lax.rsqrt(jnp.mean(o_t * o_t, axis=0, keepdims=True) + NORM_EPS)
        o_ref[pl.ds(pl.multiple_of(qi * TQ, TQ), TQ), :] = (o_t.T * out_w).astype(o_ref.dtype)
        reset(qi)
        return carry

    lax.fori_loop(0, nq, finish, 0, unroll=True)
    reset(nq)


def _attn_schedule(positions):
    b, s = positions.shape
    nq, nk = s // TQ, s // TK
    cid = jnp.right_shift(positions, CHUNK_SHIFT)
    qc = cid.reshape(b, nq, TQ)
    kc = cid.reshape(b, nk, TK)
    qmin, qmax = qc.min(-1), qc.max(-1)
    kmin, kmax = kc.min(-1), kc.max(-1)
    need = kmin[:, None, :] <= qmax[:, :, None]
    full = kmax[:, None, :] <= qmin[:, :, None]
    part = need & ~full

    def pair_list(sel):
        sel = sel.reshape(b, nq * nk)
        n = sel.sum(-1).astype(jnp.int32)
        order = jnp.argsort(~sel, axis=-1, stable=True).astype(jnp.int32)
        order = jnp.pad(order, ((0, 0), (0, PAIR_UNROLL)))
        qb = order // nk
        slot = jnp.where(jnp.arange(order.shape[1])[None, :] < n[:, None], qb, nq)
        return qb, order % nk, slot, n

    qf, kf, sf, nf = pair_list(full)
    qp, kp, sp, npart = pair_list(part)
    rows = jnp.arange(b)
    after_last_trip = (nf + PAIR_UNROLL - 1) // PAIR_UNROLL * PAIR_UNROLL
    qf = qf.at[rows, after_last_trip].set(qp[:, 0])
    kf = kf.at[rows, after_last_trip].set(kp[:, 0])
    kcid_lanes = jnp.broadcast_to(cid.reshape(b * s, 1), (b * s, LANES))
    return (qf, kf, sf, nf, qp, kp, sp, npart, cid.reshape(b, nq, 1, TQ), kcid_lanes)


def _attn(proj, vt, sched, lam_params, subw, layer, lambda_init, b, s):
    qcid, kcid = sched[8:]
    nq, nk = s // TQ, s // TK
    kern = functools.partial(_attn_kernel, lambda_init=lambda_init, nq=nq)
    grid_spec = pltpu.PrefetchScalarGridSpec(
        num_scalar_prefetch=8,
        grid=(b, N_HEADS),
        in_specs=[
            pl.BlockSpec((s, V_DIM), lambda bi, h, *_: (bi, h)),
            pl.BlockSpec((s, V_DIM), lambda bi, h, *_: (bi, N_HEADS + h)),
            pl.BlockSpec((nk, V_DIM, TK), lambda bi, h, *_: (bi, h, 0)),
            pl.BlockSpec((None, nq, 1, TQ), lambda bi, h, *_: (bi, 0, 0, 0)),
            pl.BlockSpec((s, LANES), lambda bi, h, *_: (bi, 0)),
            pl.BlockSpec((None, 4, HEAD_DIM), lambda bi, h, *_: (layer, 0, 0)),
            pl.BlockSpec((None, 1, V_DIM), lambda bi, h, *_: (layer, 0, 0)),
        ],
        out_specs=pl.BlockSpec((s, V_DIM), lambda bi, h, *_: (bi, h)),
        scratch_shapes=[
            pltpu.VMEM((2, TK, TQ), _F32),
            pltpu.VMEM((2, 1, TQ), _F32),
            pltpu.VMEM((2, nq + 1, 1, TQ), _F32),
            pltpu.VMEM((2, nq + 1, VT_ROWS, TQ), _F32),
        ],
    )
    return pl.pallas_call(
        kern,
        out_shape=jax.ShapeDtypeStruct((b * s, N_HEADS * V_DIM), _BF16),
        grid_spec=grid_spec,
        compiler_params=pltpu.CompilerParams(
            dimension_semantics=("arbitrary", "arbitrary"),
            vmem_limit_bytes=VMEM_LIMIT),
        name="attn",
    )(*sched[:8], proj, proj, vt, qcid, kcid, lam_params, subw)


def _mix_kernel(x_ref, o_ref, yb_ref, ga_ref, gb_ref, wa_ref, wb_ref, wo_ref, out_ref):
    y_b = jnp.dot(yb_ref[...], wb_ref[...], preferred_element_type=_F32)
    y_a = jnp.dot(o_ref[...], wa_ref[...], preferred_element_type=_F32)
    mixed = (jax.nn.sigmoid(ga_ref[...].astype(_F32)) * y_a
             + jax.nn.sigmoid(gb_ref[...].astype(_F32)) * y_b)
    out_ref[...] = x_ref[...] + jnp.dot(mixed.astype(_BF16), wo_ref[...], preferred_element_type=_F32)


def _mix(x, attn_o, proj, wa, wb, wo, layer):
    m = x.shape[0]

    def col(c):
        return pl.BlockSpec((TM_MIX, D_MODEL), lambda i: (i, c))

    return pl.pallas_call(
        _mix_kernel,
        out_shape=jax.ShapeDtypeStruct((m, D_MODEL), _F32),
        grid=(m // TM_MIX,),
        in_specs=[col(0), col(0), col(2), col(3), col(4), _layer_spec((D_MODEL, D_MODEL), layer),
                  _layer_spec((D_MODEL, D_MODEL), layer), _layer_spec((D_MODEL, D_MODEL), layer)],
        out_specs=col(0),
        compiler_params=pltpu.CompilerParams(
            dimension_semantics=("arbitrary",), vmem_limit_bytes=VMEM_LIMIT),
        name="mix",
    )(x, attn_o, proj, proj, proj, wa, wb, wo)


def _ffn_kernel(x_ref, nw_ref, wg_ref, wu_ref, wd_ref, fw_ref, out_ref, *, final_norm):
    x = x_ref[...]
    hn = _rms(x, nw_ref[...]).astype(_BF16)
    acc = x
    for lo, hi in zip(FF_SPLITS[:-1], FF_SPLITS[1:]):
        sl = slice(lo, hi)
        g = jnp.dot(hn, wg_ref[:, sl], preferred_element_type=_F32)
        u = jnp.dot(hn, wu_ref[:, sl], preferred_element_type=_F32)
        h = (jax.nn.silu(g) * u).astype(_BF16)
        acc = acc + jnp.dot(h, wd_ref[sl, :], preferred_element_type=_F32)
    if final_norm:
        acc = _rms(acc, fw_ref[...])
    out_ref[...] = acc


def _ffn(x, norm_w, wg, wu, wd, final_w, layer, final_norm):
    m = x.shape[0]
    tile = pl.BlockSpec((TM_FFN, D_MODEL), lambda i: (i, 0))
    kern = functools.partial(_ffn_kernel, final_norm=final_norm)
    return pl.pallas_call(
        kern,
        out_shape=jax.ShapeDtypeStruct((m, D_MODEL), _F32),
        grid=(m // TM_FFN,),
        in_specs=[tile, _layer_spec((1, D_MODEL), layer), _layer_spec((D_MODEL, D_FF), layer),
                  _layer_spec((D_MODEL, D_FF), layer), _layer_spec((D_FF, D_MODEL), layer),
                  pl.BlockSpec((1, D_MODEL), lambda i: (0, 0))],
        out_specs=tile,
        compiler_params=pltpu.CompilerParams(
            dimension_semantics=("arbitrary",), vmem_limit_bytes=VMEM_LIMIT),
        name="ffn",
    )(x, norm_w, wg, wu, wd, final_w.reshape(1, D_MODEL))


def kernel(x, positions, mix_norm, w_in, lambda_q1, lambda_k1, lambda_q2, lambda_k2, subln_w, conv_w,
           w_branch_a, w_branch_b, w_out, ffn_norm, w_gate, w_up, w_down, final_norm):
    b, s, d = x.shape
    assert d == D_MODEL and s % TQ == 0 and s % TM_MIX == 0 and (b * s) % TM_PROJ == 0
    h = x.reshape(b * s, d)
    cos_t, sin_t = _rope_tables(positions)
    sched = _attn_schedule(positions)
    mix_norm3 = mix_norm.reshape(DEPTH, 1, D_MODEL)
    ffn_norm3 = ffn_norm.reshape(DEPTH, 1, D_MODEL)
    subln3 = subln_w.reshape(DEPTH, 1, V_DIM)
    lam_params = jnp.stack([lambda_q1, lambda_k1, lambda_q2, lambda_k2], axis=1).astype(_F32)
    w_in, w_branch_a, w_branch_b, w_out, w_gate, w_up, w_down = (
        w.astype(_BF16) for w in (w_in, w_branch_a, w_branch_b, w_out, w_gate, w_up, w_down))
    for l in range(DEPTH):
        lambda_init = 0.8 - 0.6 * math.exp(-0.3 * l)
        proj, vt = _inproj(h, mix_norm3, w_in, conv_w, l, cos_t, sin_t, s)
        attn_o = _attn(proj, vt, sched, lam_params, subln3, l, lambda_init, b, s)
        h = _mix(h, attn_o, proj, w_branch_a, w_branch_b, w_out, l)
        h = _ffn(h, ffn_norm3, w_gate, w_up, w_down, final_norm, l, l == DEPTH - 1)
    return h.reshape(b, s, d)
```

```python
import functools
import math

import jax
import jax.numpy as jnp
from jax import lax
from jax.experimental import pallas as pl
from jax.experimental.pallas import tpu as pltpu

D_MODEL = 1024
DEPTH = 4
CHUNK_SHIFT = 6
N_HEADS = 8
HEAD_DIM = 64
V_DIM = 2 * HEAD_DIM
ROT_DIM = HEAD_DIM // 4
ROPE_THETA = 500000.0
CONV_K = 3
D_FF = 2816
IN_WIDTH = 8 * D_MODEL
PROJ_WIDTH = 5 * D_MODEL
NORM_EPS = 1e-6
NEG_INF = -1e30
LOG2_E = 1.4426950408889634

LANES = 128
SUBLANES = 8
BF16_ROWS = 16
VMEM_LIMIT = 56 * 1024 * 1024

TM_ROPE = 2048
TM_PROJ = 512
TN_PROJ = 512
TQ = 512
TK = 512
VT_ROWS = V_DIM + BF16_ROWS
PAIR_UNROLL = 14
PAIR_UNROLL_MASKED = 8
TM_MIX = 512
TM_FFN = 512
MXU_DIM = 256
FF_SPLITS = (0, 6 * MXU_DIM, D_FF)

_F32 = jnp.float32
_BF16 = jnp.bfloat16


def _rms(x, w):
    return x * lax.rsqrt(jnp.mean(x * x, axis=-1, keepdims=True) + NORM_EPS) * w


def _layer_spec(shape, layer):
    return pl.BlockSpec((None,) + shape, lambda *_: (layer,) + (0,) * len(shape),
                        pipeline_mode=pl.Buffered(1))


def _rope_table_kernel(ang_ref, cos_ref, sin_ref):
    ang = ang_ref[...]
    p = lax.broadcasted_iota(jnp.int32, ang.shape, 1) % HEAD_DIM
    c = jnp.cos(ang)
    s = jnp.sin(ang)
    cos_ref[...] = jnp.where(p < ROT_DIM, c, 1.0)
    sin_ref[...] = jnp.where(p < ROT_DIM // 2, -s, jnp.where(p < ROT_DIM, s, 0.0))


def _rope_tables(positions):
    m = positions.size
    freqs = ROPE_THETA ** (-jnp.arange(0, ROT_DIM, 2, dtype=_F32) / ROT_DIM)
    lane_freq = jnp.tile(freqs, LANES // (ROT_DIM // 2))
    ang = positions.reshape(m, 1).astype(_F32) * lane_freq[None, :]
    spec = pl.BlockSpec((TM_ROPE, LANES), lambda i: (i, 0))
    return pl.pallas_call(
        _rope_table_kernel,
        out_shape=(jax.ShapeDtypeStruct((m, LANES), _F32),) * 2,
        grid=(m // TM_ROPE,),
        in_specs=[spec],
        out_specs=(spec, spec),
        name="rope_tables",
    )(ang)


def _inproj_kernel(x_ref, nw_ref, w_ref, cos_ref, sin_ref, cw_ref, o_ref, vt_ref, ext_ref, carry_ref,
                   *, tiles_per_seq):
    i = pl.program_id(0)

    @pl.when(i == 0)
    def _():
        carry_ref[...] = jnp.zeros(carry_ref.shape, _F32)

    xn = _rms(x_ref[...], nw_ref[...]).astype(_BF16)

    def proj(lo):
        return jnp.dot(xn, w_ref[:, lo:lo + TN_PROJ], preferred_element_type=_F32)

    cos = cos_ref[...]
    sin = sin_ref[...]
    first_half = (lax.broadcasted_iota(jnp.int32, cos.shape, 1) % HEAD_DIM) < ROT_DIM // 2
    q_scale = HEAD_DIM ** -0.5 * LOG2_E
    for lo in range(0, 2 * D_MODEL, TN_PROJ):
        acc = proj(lo)
        cs, sn = (cos * q_scale, sin * q_scale) if lo < D_MODEL else (cos, sin)
        for c in range(TN_PROJ // LANES):
            t = acc[:, c * LANES:(c + 1) * LANES]
            partner = jnp.where(first_half,
                                pltpu.roll(t, LANES - ROT_DIM // 2, axis=1),
                                pltpu.roll(t, ROT_DIM // 2, axis=1))
            o_ref[:, lo + c * LANES:lo + (c + 1) * LANES] = (t * cs + partner * sn).astype(o_ref.dtype)

    for lo in range(0, D_MODEL, TN_PROJ):
        vt_ref[0, lo:lo + TN_PROJ, :] = proj(2 * D_MODEL + lo).T.astype(vt_ref.dtype)

    seq_start = i % tiles_per_seq == 0
    for n, lo in enumerate(range(0, D_MODEL, TN_PROJ)):
        cols = slice(lo, lo + TN_PROJ)
        u = proj(5 * D_MODEL + lo)
        cu = proj(4 * D_MODEL + lo) * u
        ext_ref[n, 0:SUBLANES, :] = jnp.where(seq_start, 0.0, carry_ref[:, cols])
        ext_ref[n, SUBLANES:, :] = cu
        carry_ref[:, cols] = cu[TM_PROJ - SUBLANES:, :]
        cw = cw_ref[:, cols]
        conv = (cw[0:1] * ext_ref[n, SUBLANES - 2:SUBLANES - 2 + TM_PROJ, :]
                + cw[1:2] * ext_ref[n, SUBLANES - 1:SUBLANES - 1 + TM_PROJ, :]
                + cw[2:3] * ext_ref[n, SUBLANES:, :])
        b_gate = proj(3 * D_MODEL + lo)
        o_ref[:, 2 * D_MODEL + lo:2 * D_MODEL + lo + TN_PROJ] = (b_gate * conv).astype(o_ref.dtype)

    for lo in range(0, 2 * D_MODEL, TN_PROJ):
        o_ref[:, 3 * D_MODEL + lo:3 * D_MODEL + lo + TN_PROJ] = proj(6 * D_MODEL + lo).astype(o_ref.dtype)


def _inproj(x, norm_w, w_bf16, conv_w, layer, cos_t, sin_t, s):
    m = x.shape[0]
    assert TM_PROJ == TK and s % TM_PROJ == 0
    kern = functools.partial(_inproj_kernel, tiles_per_seq=s // TM_PROJ)
    return pl.pallas_call(
        kern,
        out_shape=(jax.ShapeDtypeStruct((m, PROJ_WIDTH), _BF16),
                   jax.ShapeDtypeStruct((m // TK, D_MODEL, TK), _BF16)),
        grid=(m // TM_PROJ,),
        in_specs=[
            pl.BlockSpec((TM_PROJ, D_MODEL), lambda i: (i, 0)),
            _layer_spec((1, D_MODEL), layer),
            _layer_spec((D_MODEL, IN_WIDTH), layer),
            pl.BlockSpec((TM_PROJ, LANES), lambda i: (i, 0)),
            pl.BlockSpec((TM_PROJ, LANES), lambda i: (i, 0)),
            _layer_spec((CONV_K, D_MODEL), layer),
        ],
        out_specs=(pl.BlockSpec((TM_PROJ, PROJ_WIDTH), lambda i: (i, 0)),
                   pl.BlockSpec((1, D_MODEL, TK), lambda i: (i, 0, 0))),
        scratch_shapes=[pltpu.VMEM((D_MODEL // TN_PROJ, TM_PROJ + SUBLANES, TN_PROJ), _F32),
                        pltpu.VMEM((SUBLANES, D_MODEL), _F32)],
        compiler_params=pltpu.CompilerParams(
            dimension_semantics=("arbitrary",), vmem_limit_bytes=VMEM_LIMIT),
        name="inproj",
    )(x, norm_w, w_bf16, cos_t, sin_t, conv_w)


def _attn_kernel(qf_ref, kf_ref, sf_ref, nf_ref, qp_ref, kp_ref, sp_ref, np_ref,
                 q_ref, k_ref, vt_ref, qcid_ref, kcid_ref, lam_ref, subw_ref, o_ref,
                 s_ref, mb_ref, m_ref, acc_ref, *, lambda_init, nq):
    b = pl.program_id(0)

    @pl.when((b == 0) & (pl.program_id(1) == 0))
    def _():
        m_ref[...] = jnp.full(m_ref.shape, NEG_INF, _F32)
        acc_ref[...] = jnp.zeros(acc_ref.shape, _F32)

    lane = lax.broadcasted_iota(jnp.int32, (TQ, V_DIM), 1)
    ones_rows = jnp.ones((VT_ROWS - V_DIM, TK), _BF16)

    def load_pair(qi, ki, masked):
        q = q_ref[pl.ds(pl.multiple_of(qi * TQ, TQ), TQ), :]
        zero = jnp.zeros_like(q)
        qms = (jnp.where(lane < HEAD_DIM, q, zero), jnp.where(lane >= HEAD_DIM, q, zero))
        ks = pl.multiple_of(ki * TK, TK)
        k = k_ref[pl.ds(ks, TK), :]
        cids = None
        if masked:
            kc = kcid_ref[pl.ds(ks, TK), :]
            cids = (jnp.concatenate([kc] * (TQ // LANES), axis=1), qcid_ref[qi])
        return k, qms, cids

    def scores(j, k, qms, cids):
        s = lax.dot_general(k, qms[j], (((1,), (1,)), ((), ())), preferred_element_type=_F32)
        if cids is not None:
            kc, qc = cids
            s = jnp.where(kc <= qc, s, NEG_INF) if j == 0 else jnp.where(kc > qc, NEG_INF, s)
        s_ref[j] = s
        mb_ref[j] = jnp.max(s, axis=0, keepdims=True)

    def first_scores(ql_ref, kl_ref, masked):
        k0, qms0, allowed0 = load_pair(ql_ref[b, 0], kl_ref[b, 0], masked)
        for j in range(2):
            scores(j, k0, qms0, allowed0)

    def run_list(ql_ref, kl_ref, sl_ref, n, masked, unroll):
        @pl.when(n > 0)
        def _():
            def body(i, carry):
                for u in range(unroll):
                    t = i * unroll + u
                    slot = sl_ref[b, t]
                    vt = jnp.concatenate([vt_ref[kl_ref[b, t]], ones_rows], axis=0)
                    kn, qmsn, allowedn = load_pair(ql_ref[b, t + 1], kl_ref[b, t + 1],
                                                   masked or u == unroll - 1)
                    for j in range(2):
                        m_old = m_ref[j, slot]
                        m_new = jnp.maximum(m_old, mb_ref[j])
                        alpha = jnp.exp2(m_old - m_new)
                        p = jnp.exp2(s_ref[j] - m_new).astype(_BF16)
                        scores(j, kn, qmsn, allowedn)
                        acc_ref[j, slot] = (alpha * acc_ref[j, slot]
                                            + jnp.dot(vt, p, preferred_element_type=_F32))
                        m_ref[j, slot] = m_new
                return carry

            lax.fori_loop(0, (n + unroll - 1) // unroll, body, 0)

    n_full = nf_ref[b]

    @pl.when(n_full > 0)
    def _():
        first_scores(qf_ref, kf_ref, masked=False)

    @pl.when(n_full == 0)
    def _():
        first_scores(qp_ref, kp_ref, masked=True)

    run_list(qf_ref, kf_ref, sf_ref, n_full, masked=False, unroll=PAIR_UNROLL)
    run_list(qp_ref, kp_ref, sp_ref, np_ref[b], masked=True, unroll=PAIR_UNROLL_MASKED)

    lam_v = lam_ref[...]
    lam = (jnp.exp(jnp.sum(lam_v[0:1] * lam_v[1:2], axis=-1, keepdims=True))
           - jnp.exp(jnp.sum(lam_v[2:3] * lam_v[3:4], axis=-1, keepdims=True))
           + lambda_init)
    out_w = subw_ref[...] * (1.0 - lambda_init)

    def reset(slot):
        for j in range(2):
            m_ref[j, slot] = jnp.full((1, TQ), NEG_INF, _F32)
            acc_ref[j, slot] = jnp.zeros((VT_ROWS, TQ), _F32)

    def finish(qi, carry):
        a1 = acc_ref[0, qi]
        a2 = acc_ref[1, qi]
        o_t = (a1[0:V_DIM] * (1.0 / a1[V_DIM:V_DIM + 1])
               - a2[0:V_DIM] * (lam / a2[V_DIM:V_DIM + 1]))
        o_t = o_t * lax.rsqrt(jnp.mean(o_t * o_t, axis=0, keepdims=True) + NORM_EPS)
        o_ref[pl.ds(pl.multiple_of(qi * TQ, TQ), TQ), :] = (o_t.T * out_w).astype(o_ref.dtype)
        reset(qi)
        return carry

    lax.fori_loop(0, nq, finish, 0, unroll=True)
    reset(nq)


def _attn_schedule(positions):
    b, s = positions.shape
    nq, nk = s // TQ, s // TK
    cid = jnp.right_shift(positions, CHUNK_SHIFT)
    qc = cid.reshape(b, nq, TQ)
    kc = cid.reshape(b, nk, TK)
    qmin, qmax = qc.min(-1), qc.max(-1)
    kmin, kmax = kc.min(-1), kc.max(-1)
    need = kmin[:, None, :] <= qmax[:, :, None]
    full = kmax[:, None, :] <= qmin[:, :, None]
    part = need & ~full

    def pair_list(sel):
        sel = sel.reshape(b, nq * nk)
        n = sel.sum(-1).astype(jnp.int32)
        order = jnp.argsort(~sel, axis=-1, stable=True).astype(jnp.int32)
        order = jnp.pad(order, ((0, 0), (0, PAIR_UNROLL)))
        qb = order // nk
        slot = jnp.where(jnp.arange(order.shape[1])[None, :] < n[:, None], qb, nq)
        return qb, order % nk, slot, n

    qf, kf, sf, nf = pair_list(full)
    qp, kp, sp, npart = pair_list(part)
    rows = jnp.arange(b)
    after_last_trip = (nf + PAIR_UNROLL - 1) // PAIR_UNROLL * PAIR_UNROLL
    qf = qf.at[rows, after_last_trip].set(qp[:, 0])
    kf = kf.at[rows, after_last_trip].set(kp[:, 0])
    kcid_lanes = jnp.broadcast_to(cid.reshape(b * s, 1), (b * s, LANES))
    return (qf, kf, sf, nf, qp, kp, sp, npart, cid.reshape(b, nq, 1, TQ), kcid_lanes)


def _attn(proj, vt, sched, lam_params, subw, layer, lambda_init, b, s):
    qcid, kcid = sched[8:]
    nq, nk = s // TQ, s // TK
    kern = functools.partial(_attn_kernel, lambda_init=lambda_init, nq=nq)
    grid_spec = pltpu.PrefetchScalarGridSpec(
        num_scalar_prefetch=8,
        grid=(b, N_HEADS),
        in_specs=[
            pl.BlockSpec((s, V_DIM), lambda bi, h, *_: (bi, h)),
            pl.BlockSpec((s, V_DIM), lambda bi, h, *_: (bi, N_HEADS + h)),
            pl.BlockSpec((nk, V_DIM, TK), lambda bi, h, *_: (bi, h, 0)),
            pl.BlockSpec((None, nq, 1, TQ), lambda bi, h, *_: (bi, 0, 0, 0)),
            pl.BlockSpec((s, LANES), lambda bi, h, *_: (bi, 0)),
            pl.BlockSpec((None, 4, HEAD_DIM), lambda bi, h, *_: (layer, 0, 0)),
            pl.BlockSpec((None, 1, V_DIM), lambda bi, h, *_: (layer, 0, 0)),
        ],
        out_specs=pl.BlockSpec((s, V_DIM), lambda bi, h, *_: (bi, h)),
        scratch_shapes=[
            pltpu.VMEM((2, TK, TQ), _F32),
            pltpu.VMEM((2, 1, TQ), _F32),
            pltpu.VMEM((2, nq + 1, 1, TQ), _F32),
            pltpu.VMEM((2, nq + 1, VT_ROWS, TQ), _F32),
        ],
    )
    return pl.pallas_call(
        kern,
        out_shape=jax.ShapeDtypeStruct((b * s, N_HEADS * V_DIM), _BF16),
        grid_spec=grid_spec,
        compiler_params=pltpu.CompilerParams(
            dimension_semantics=("arbitrary", "arbitrary"),
            vmem_limit_bytes=VMEM_LIMIT),
        name="attn",
    )(*sched[:8], proj, proj, vt, qcid, kcid, lam_params, subw)


def _mix_kernel(x_ref, o_ref, yb_ref, ga_ref, gb_ref, wa_ref, wb_ref, wo_ref, out_ref):
    y_b = jnp.dot(yb_ref[...], wb_ref[...], preferred_element_type=_F32)
    y_a = jnp.dot(o_ref[...], wa_ref[...], preferred_element_type=_F32)
    mixed = (jax.nn.sigmoid(ga_ref[...].astype(_F32)) * y_a
             + jax.nn.sigmoid(gb_ref[...].astype(_F32)) * y_b)
    out_ref[...] = x_ref[...] + jnp.dot(mixed.astype(_BF16), wo_ref[...], preferred_element_type=_F32)


def _mix(x, attn_o, proj, wa, wb, wo, layer):
    m = x.shape[0]

    def col(c):
        return pl.BlockSpec((TM_MIX, D_MODEL), lambda i: (i, c))

    return pl.pallas_call(
        _mix_kernel,
        out_shape=jax.ShapeDtypeStruct((m, D_MODEL), _F32),
        grid=(m // TM_MIX,),
        in_specs=[col(0), col(0), col(2), col(3), col(4), _layer_spec((D_MODEL, D_MODEL), layer),
                  _layer_spec((D_MODEL, D_MODEL), layer), _layer_spec((D_MODEL, D_MODEL), layer)],
        out_specs=col(0),
        compiler_params=pltpu.CompilerParams(
            dimension_semantics=("arbitrary",), vmem_limit_bytes=VMEM_LIMIT),
        name="mix",
    )(x, attn_o, proj, proj, proj, wa, wb, wo)


def _ffn_kernel(x_ref, nw_ref, wg_ref, wu_ref, wd_ref, fw_ref, out_ref, *, final_norm):
    x = x_ref[...]
    hn = _rms(x, nw_ref[...]).astype(_BF16)
    acc = x
    for lo, hi in zip(FF_SPLITS[:-1], FF_SPLITS[1:]):
        sl = slice(lo, hi)
        g = jnp.dot(hn, wg_ref[:, sl], preferred_element_type=_F32)
        u = jnp.dot(hn, wu_ref[:, sl], preferred_element_type=_F32)
        h = (jax.nn.silu(g) * u).astype(_BF16)
        acc = acc + jnp.dot(h, wd_ref[sl, :], preferred_element_type=_F32)
    if final_norm:
        acc = _rms(acc, fw_ref[...])
    out_ref[...] = acc


def _ffn(x, norm_w, wg, wu, wd, final_w, layer, final_norm):
    m = x.shape[0]
    tile = pl.BlockSpec((TM_FFN, D_MODEL), lambda i: (i, 0))
    kern = functools.partial(_ffn_kernel, final_norm=final_norm)
    return pl.pallas_call(
        kern,
        out_shape=jax.ShapeDtypeStruct((m, D_MODEL), _F32),
        grid=(m // TM_FFN,),
        in_specs=[tile, _layer_spec((1, D_MODEL), layer), _layer_spec((D_MODEL, D_FF), layer),
                  _layer_spec((D_MODEL, D_FF), layer), _layer_spec((D_FF, D_MODEL), layer),
                  pl.BlockSpec((1, D_MODEL), lambda i: (0, 0))],
        out_specs=tile,
        compiler_params=pltpu.CompilerParams(
            dimension_semantics=("arbitrary",), vmem_limit_bytes=VMEM_LIMIT),
        name="ffn",
    )(x, norm_w, wg, wu, wd, final_w.reshape(1, D_MODEL))


def kernel(x, positions, mix_norm, w_in, lambda_q1, lambda_k1, lambda_q2, lambda_k2, subln_w, conv_w,
           w_branch_a, w_branch_b, w_out, ffn_norm, w_gate, w_up, w_down, final_norm):
    b, s, d = x.shape
    assert d == D_MODEL and s % TQ == 0 and s % TM_MIX == 0 and (b * s) % TM_PROJ == 0
    h = x.reshape(b * s, d)
    cos_t, sin_t = _rope_tables(positions)
    sched = _attn_schedule(positions)
    mix_norm3 = mix_norm.reshape(DEPTH, 1, D_MODEL)
    ffn_norm3 = ffn_norm.reshape(DEPTH, 1, D_MODEL)
    subln3 = subln_w.reshape(DEPTH, 1, V_DIM)
    lam_params = jnp.stack([lambda_q1, lambda_k1, lambda_q2, lambda_k2], axis=1).astype(_F32)
    w_in, w_branch_a, w_branch_b, w_out, w_gate, w_up, w_down = (
        w.astype(_BF16) for w in (w_in, w_branch_a, w_branch_b, w_out, w_gate, w_up, w_down))
    for l in range(DEPTH):
        lambda_init = 0.8 - 0.6 * math.exp(-0.3 * l)
        proj, vt = _inproj(h, mix_norm3, w_in, conv_w, l, cos_t, sin_t, s)
        attn_o = _attn(proj, vt, sched, lam_params, subln3, l, lambda_init, b, s)
        h = _mix(h, attn_o, proj, w_branch_a, w_branch_b, w_out, l)
        h = _ffn(h, ffn_norm3, w_gate, w_up, w_down, final_norm, l, l == DEPTH - 1)
    return h.reshape(b, s, d)
```
